```python
import math
import jax, jax.numpy as jnp
from jax import lax
import numpy as np

D_MODEL = 2048
BATCH = 4
SEQ = 2048
DEPTH = 1
DEC_BATCH = 128
DEC_SEQ = 8
PAST_LEN = 16384
PAGE_SIZE = 128

D_CONV = D_MODEL // 2
D_SSM = D_MODEL - D_CONV
CONV_WIDTH = 31
SSM_GROUP_CH = 16
N_SSM_GROUPS = D_SSM // SSM_GROUP_CH
SSM_STATE = 64
D_FF = ((8 * D_MODEL // 3 + 255) // 256) * 256
D_IN = 2 * D_CONV + D_SSM
EPS = 1e-6
DT_MIN = 1e-3
DT_MAX = 1e-1

kernel_name = "hymba_conformer_conv_s5_decode_step"


def rmsnorm(x, g):
    xf = x.astype(jnp.float32)
    y = xf * lax.rsqrt(jnp.mean(xf * xf, axis=-1, keepdims=True) + EPS) * g.astype(jnp.float32)
    return y.astype(x.dtype)


def conformer_conv(v, gate, conv_buf, conv_w, conv_b, ln_g, ln_b):
    u = v * jax.nn.sigmoid(gate)
    up = jnp.concatenate([conv_buf.astype(u.dtype), u], axis=1)
    y = lax.conv_general_dilated(
        up, conv_w.astype(u.dtype)[:, None, :], window_strides=(1,), padding='VALID',
        dimension_numbers=('NWC', 'WIO', 'NWC'), feature_group_count=D_CONV)
    y = y.astype(jnp.float32) + conv_b.astype(jnp.float32)
    mu = jnp.mean(y, axis=-1, keepdims=True)
    yc = y - mu
    var = jnp.mean(yc * yc, axis=-1, keepdims=True)
    y = yc * lax.rsqrt(var + EPS) * ln_g.astype(jnp.float32) + ln_b.astype(jnp.float32)
    y = jax.nn.silu(y)
    new_buf = up[:, -(CONV_WIDTH - 1):]
    return y.astype(v.dtype), new_buf


def _complex_scan_op(e1, e2):
    a1r, a1i, b1r, b1i = e1
    a2r, a2i, b2r, b2i = e2
    ar = a2r * a1r - a2i * a1i
    ai = a2r * a1i + a2i * a1r
    br = a2r * b1r - a2i * b1i + b2r
    bi = a2r * b1i + a2i * b1r + b2i
    return (ar, ai, br, bi)


def s5_mixer(u, h_re, h_im, A_re, A_im, log_dt, B_re, B_im, C_re, C_im, D, w_glu):
    b, l, _ = u.shape
    f32 = jnp.float32
    uf = u.astype(f32).reshape(b, l, N_SSM_GROUPS, SSM_GROUP_CH)
    Ar = A_re.astype(f32); Ai = A_im.astype(f32)
    dt = jnp.exp(log_dt.astype(f32))[:, None]
    mag = jnp.exp(dt * Ar)
    ab_re = mag * jnp.cos(dt * Ai)
    ab_im = mag * jnp.sin(dt * Ai)
    den = Ar * Ar + Ai * Ai
    nr = ab_re - 1.0
    ni = ab_im
    k_re = (nr * Ar + ni * Ai) / den
    k_im = (ni * Ar - nr * Ai) / den
    Br = B_re.astype(f32); Bi = B_im.astype(f32)
    Bb_re = k_re[..., None] * Br - k_im[..., None] * Bi
    Bb_im = k_re[..., None] * Bi + k_im[..., None] * Br
    bu_re = jnp.einsum('blgc,gpc->blgp', uf, Bb_re)
    bu_im = jnp.einsum('blgc,gpc->blgp', uf, Bb_im)
    hr = h_re.astype(f32); hi = h_im.astype(f32)
    bu_re = bu_re.at[:, 0].add(ab_re * hr - ab_im * hi)
    bu_im = bu_im.at[:, 0].add(ab_re * hi + ab_im * hr)
    a_re = jnp.broadcast_to(ab_re, bu_re.shape)
    a_im = jnp.broadcast_to(ab_im, bu_re.shape)
    _, _, hs_re, hs_im = lax.associative_scan(_complex_scan_op, (a_re, a_im, bu_re, bu_im), axis=1)
    y = (jnp.einsum('blgp,gcp->blgc', hs_re, C_re.astype(f32))
         - jnp.einsum('blgp,gcp->blgc', hs_im, C_im.astype(f32)))
    y = y + D.astype(f32).reshape(N_SSM_GROUPS, SSM_GROUP_CH) * uf
    y = jax.nn.gelu(y.reshape(b, l, D_SSM))
    y = y * jax.nn.sigmoid(y @ w_glu.astype(f32))
    return y.astype(u.dtype), hs_re[:, -1], hs_im[:, -1]


def hybrid_layer(x, conv_buf, h_re, h_im, norm_mix, w_in, conv_w, conv_b, conv_ln_g, conv_ln_b,
                 ssm_A_re, ssm_A_im, ssm_log_dt, ssm_B_re, ssm_B_im, ssm_C_re, ssm_C_im, ssm_D,
                 w_glu, gnorm_conv, gnorm_ssm, w_out, norm_ffn, w_ffn_gate, w_ffn_up, w_ffn_down):
    xn = rmsnorm(x, norm_mix)
    proj = xn @ w_in
    c_val = proj[..., :D_CONV]
    c_gate = proj[..., D_CONV:2 * D_CONV]
    s_in = proj[..., 2 * D_CONV:]
    yc, new_buf = conformer_conv(c_val, c_gate, conv_buf, conv_w, conv_b, conv_ln_g, conv_ln_b)
    ys, nh_re, nh_im = s5_mixer(s_in, h_re, h_im, ssm_A_re, ssm_A_im, ssm_log_dt,
                                ssm_B_re, ssm_B_im, ssm_C_re, ssm_C_im, ssm_D, w_glu)
    mix = jnp.concatenate([rmsnorm(yc, gnorm_conv), rmsnorm(ys, gnorm_ssm)], axis=-1) @ w_out
    h = x + mix.astype(x.dtype)
    hn = rmsnorm(h, norm_ffn)
    ff = (jax.nn.silu(hn @ w_ffn_gate) * (hn @ w_ffn_up)) @ w_ffn_down
    h = h + ff.astype(x.dtype)
    return h, new_buf, nh_re, nh_im


def setup_inputs(seed: int = 0) -> dict:
    key = jax.random.key(seed)
    ks = jax.random.split(key, 32)
    f32 = jnp.float32
    nrm = lambda k, shape, s: jax.random.normal(k, shape, f32) * s
    L = DEPTH
    x_prompt = nrm(ks[0], (BATCH, SEQ, D_MODEL), 1.0)
    x_sample = nrm(ks[1], (DEC_BATCH, DEC_SEQ, D_MODEL), 1.0)
    state_conv = nrm(ks[2], (L, DEC_BATCH, CONV_WIDTH - 1, D_CONV), 0.5)
    state_ssm_re = nrm(ks[3], (L, DEC_BATCH, N_SSM_GROUPS, SSM_STATE), 0.5)
    state_ssm_im = nrm(ks[4], (L, DEC_BATCH, N_SSM_GROUPS, SSM_STATE), 0.5)
    norm_mix = 1.0 + nrm(ks[5], (L, D_MODEL), 0.02)
    w_in = nrm(ks[6], (L, D_MODEL, D_IN), D_MODEL ** -0.5)
    conv_w = nrm(ks[7], (L, CONV_WIDTH, D_CONV), CONV_WIDTH ** -0.5)
    conv_b = nrm(ks[8], (L, D_CONV), 0.02)
    conv_ln_g = 1.0 + nrm(ks[9], (L, D_CONV), 0.02)
    conv_ln_b = nrm(ks[10], (L, D_CONV), 0.02)
    n = jnp.arange(SSM_STATE, dtype=f32)
    ssm_A_re = -0.5 * jnp.exp(nrm(ks[11], (L, N_SSM_GROUPS, SSM_STATE), 0.01))
    ssm_A_im = math.pi * n + nrm(ks[12], (L, N_SSM_GROUPS, SSM_STATE), 0.01)
    ssm_log_dt = jax.random.uniform(ks[13], (L, N_SSM_GROUPS), f32,
                                    math.log(DT_MIN), math.log(DT_MAX))
    ssm_B_re = nrm(ks[14], (L, N_SSM_GROUPS, SSM_STATE, SSM_GROUP_CH), (2 * SSM_GROUP_CH) ** -0.5)
    ssm_B_im = nrm(ks[15], (L, N_SSM_GROUPS, SSM_STATE, SSM_GROUP_CH), (2 * SSM_GROUP_CH) ** -0.5)
    ssm_C_re = nrm(ks[16], (L, N_SSM_GROUPS, SSM_GROUP_CH, SSM_STATE), (2 * SSM_STATE) ** -0.5)
    ssm_C_im = nrm(ks[17], (L, N_SSM_GROUPS, SSM_GROUP_CH, SSM_STATE), (2 * SSM_STATE) ** -0.5)
    ssm_D = nrm(ks[18], (L, D_SSM), 1.0)
    w_glu = nrm(ks[19], (L, D_SSM, D_SSM), D_SSM ** -0.5)
    gnorm_conv = 1.0 + nrm(ks[20], (L, D_CONV), 0.02)
    gnorm_ssm = 1.0 + nrm(ks[21], (L, D_SSM), 0.02)
    w_out = nrm(ks[22], (L, D_MODEL, D_MODEL), D_MODEL ** -0.5)
    norm_ffn = 1.0 + nrm(ks[23], (L, D_MODEL), 0.02)
    w_ffn_gate = nrm(ks[24], (L, D_MODEL, D_FF), D_MODEL ** -0.5)
    w_ffn_up = nrm(ks[25], (L, D_MODEL, D_FF), D_MODEL ** -0.5)
    w_ffn_down = nrm(ks[26], (L, D_FF, D_MODEL), D_FF ** -0.5)
    norm_final = 1.0 + nrm(ks[27], (D_MODEL,), 0.02)
    return {"x_prompt": x_prompt, "x_sample": x_sample, "state_conv": state_conv,
            "state_ssm_re": state_ssm_re, "state_ssm_im": state_ssm_im,
            "norm_mix": norm_mix, "w_in": w_in, "conv_w": conv_w, "conv_b": conv_b,
            "conv_ln_g": conv_ln_g, "conv_ln_b": conv_ln_b, "ssm_A_re": ssm_A_re,
            "ssm_A_im": ssm_A_im, "ssm_log_dt": ssm_log_dt, "ssm_B_re": ssm_B_re,
            "ssm_B_im": ssm_B_im, "ssm_C_re": ssm_C_re, "ssm_C_im": ssm_C_im, "ssm_D": ssm_D,
            "w_glu": w_glu, "gnorm_conv": gnorm_conv, "gnorm_ssm": gnorm_ssm, "w_out": w_out,
            "norm_ffn": norm_ffn, "w_ffn_gate": w_ffn_gate, "w_ffn_up": w_ffn_up,
            "w_ffn_down": w_ffn_down, "norm_final": norm_final}


def reference(x_prompt, x_sample, state_conv, state_ssm_re, state_ssm_im,
              norm_mix, w_in, conv_w, conv_b, conv_ln_g, conv_ln_b,
              ssm_A_re, ssm_A_im, ssm_log_dt, ssm_B_re, ssm_B_im, ssm_C_re, ssm_C_im, ssm_D,
              w_glu, gnorm_conv, gnorm_ssm, w_out, norm_ffn, w_ffn_gate, w_ffn_up, w_ffn_down,
              norm_final):
    hp = x_prompt
    hs = x_sample
    pc, pr, pi_, sc, sr, si = [], [], [], [], [], []
    for d in range(DEPTH):
        params = (norm_mix[d], w_in[d], conv_w[d], conv_b[d], conv_ln_g[d], conv_ln_b[d],
                  ssm_A_re[d], ssm_A_im[d], ssm_log_dt[d], ssm_B_re[d], ssm_B_im[d],
                  ssm_C_re[d], ssm_C_im[d], ssm_D[d], w_glu[d], gnorm_conv[d], gnorm_ssm[d],
                  w_out[d], norm_ffn[d], w_ffn_gate[d], w_ffn_up[d], w_ffn_down[d])
        b = hp.shape[0]
        buf0 = jnp.zeros((b, CONV_WIDTH - 1, D_CONV), hp.dtype)
        h0 = jnp.zeros((b, N_SSM_GROUPS, SSM_STATE), jnp.float32)
        hp, nbuf, nr, ni = hybrid_layer(hp, buf0, h0, h0, *params)
        pc.append(nbuf); pr.append(nr); pi_.append(ni)
        hs, nbuf, nr, ni = hybrid_layer(hs, state_conv[d], state_ssm_re[d], state_ssm_im[d], *params)
        sc.append(nbuf); sr.append(nr); si.append(ni)
    y_prompt = rmsnorm(hp, norm_final)
    y_sample = rmsnorm(hs, norm_final)
    new_conv_prompt = jnp.stack(pc, axis=0)
    new_ssm_re_prompt = jnp.stack(pr, axis=0)
    new_ssm_im_prompt = jnp.stack(pi_, axis=0)
    new_conv_sample = jnp.stack(sc, axis=0)
    new_ssm_re_sample = jnp.stack(sr, axis=0)
    new_ssm_im_sample = jnp.stack(si, axis=0)
    return (y_prompt, y_sample, new_conv_prompt, new_ssm_re_prompt, new_ssm_im_prompt,
            new_conv_sample, new_ssm_re_sample, new_ssm_im_sample)
```

```python
import functools
import math

import jax
import jax.numpy as jnp
from jax import lax
from jax.experimental import pallas as pl
from jax.experimental.pallas import tpu as pltpu

F32 = jnp.float32
BF16 = jnp.bfloat16

D_MODEL = 2048
D_CONV = 1024
D_SSM = 1024
D_IN = 2 * D_CONV + D_SSM
CONV_WIDTH = 31
HIST = CONV_WIDTH - 1
SSM_GROUP_CH = 16
N_GROUPS = D_SSM // SSM_GROUP_CH
SSM_STATE = 64
N_STATE = N_GROUPS * SSM_STATE
D_FF = 5632
EPS = 1e-6

SUBLANES = 8
CONV_PAD = 32
STATE_TILE = 256
N_STATE_TILES = N_STATE // STATE_TILE
B_KSLICE = 128
OUT_TILE = 256
N_OUT_TILES = D_SSM // OUT_TILE
STATES_PER_OUT_TILE = N_STATE // N_OUT_TILES
VMEM_LIMIT = 56 * 1024 * 1024

T_A1R, T_A1I, T_A2R, T_A2I, T_A4R, T_A4I, T_PWR, T_PWI, T_AR, T_AI = range(10)


def _sigmoid(x):
    return 1.0 / (1.0 + jnp.exp(-x))


def _rms_scale(x):
    return lax.rsqrt(jnp.mean(x * x, axis=-1, keepdims=True) + EPS)


def _gelu_tanh(x):
    c = math.sqrt(2.0 / math.pi)
    return x * (0.5 * (1.0 + jnp.tanh(c * (x + 0.044715 * (x * x * x)))))


def _in_proj_kernel(x_ref, g_ref, w_ref, o_ref, xn_ref):
    @pl.when(pl.program_id(1) == 0)
    def _():
        x = x_ref[...]
        xn_ref[...] = (x * _rms_scale(x) * g_ref[...]).astype(BF16)

    o_ref[...] = jnp.dot(xn_ref[...], w_ref[...], preferred_element_type=F32)


def _in_proj(x, g, w, tm):
    rows = x.shape[0]
    tn = 1024
    return pl.pallas_call(
        _in_proj_kernel,
        out_shape=jax.ShapeDtypeStruct((rows, D_IN), F32),
        grid=(rows // tm, D_IN // tn),
        in_specs=[
            pl.BlockSpec((tm, D_MODEL), lambda i, j: (i, 0)),
            pl.BlockSpec((1, D_MODEL), lambda i, j: (0, 0)),
            pl.BlockSpec((D_MODEL, tn), lambda i, j: (0, j)),
        ],
        out_specs=pl.BlockSpec((tm, tn), lambda i, j: (i, j)),
        scratch_shapes=[pltpu.VMEM((tm, D_MODEL), BF16)],
        compiler_params=pltpu.CompilerParams(
            dimension_semantics=("arbitrary", "arbitrary"), vmem_limit_bytes=VMEM_LIMIT),
        name="in_proj",
    )(x, g, w)


def _ffn_kernel(mix_ref, x_ref, wo_ref, nf_ref, wg_ref, wu_ref, wd_ref, nfin_ref, o_ref, hn_ref,
                *, final_norm):
    f = pl.program_id(1)

    @pl.when(f == 0)
    def _():
        h = x_ref[...] + jnp.dot(mix_ref[...], wo_ref[...], preferred_element_type=F32)
        o_ref[...] = h
        hn_ref[...] = (h * _rms_scale(h) * nf_ref[...]).astype(BF16)

    hn = hn_ref[...]
    gate = jnp.dot(hn, wg_ref[...], preferred_element_type=F32)
    up = jnp.dot(hn, wu_ref[...], preferred_element_type=F32)
    act = (gate * _sigmoid(gate)) * up
    o_ref[...] += jnp.dot(act.astype(BF16), wd_ref[...], preferred_element_type=F32)

    if final_norm:
        @pl.when(f == pl.num_programs(1) - 1)
        def _():
            h = o_ref[...]
            o_ref[...] = h * _rms_scale(h) * nfin_ref[...]


def _ffn(mix, x, wo, nf, wg, wu, wd, nfin, tm, final_norm):
    rows = x.shape[0]
    tf = 512
    return pl.pallas_call(
        functools.partial(_ffn_kernel, final_norm=final_norm),
        out_shape=jax.ShapeDtypeStruct((rows, D_MODEL), F32),
        grid=(rows // tm, D_FF // tf),
        in_specs=[
            pl.BlockSpec((tm, D_MODEL), lambda i, f: (i, 0)),
            pl.BlockSpec((tm, D_MODEL), lambda i, f: (i, 0)),
            pl.BlockSpec((D_MODEL, D_MODEL), lambda i, f: (0, 0)),
            pl.BlockSpec((1, D_MODEL), lambda i, f: (0, 0)),
            pl.BlockSpec((D_MODEL, tf), lambda i, f: (0, f)),
            pl.BlockSpec((D_MODEL, tf), lambda i, f: (0, f)),
            pl.BlockSpec((tf, D_MODEL), lambda i, f: (f, 0)),
            pl.BlockSpec((1, D_MODEL), lambda i, f: (0, 0)),
        ],
        out_specs=pl.BlockSpec((tm, D_MODEL), lambda i, f: (i, 0)),
        scratch_shapes=[pltpu.VMEM((tm, D_MODEL), BF16)],
        compiler_params=pltpu.CompilerParams(
            dimension_semantics=("arbitrary", "arbitrary"), vmem_limit_bytes=VMEM_LIMIT),
        name="ffn",
    )(mix, x, wo, nf, wg, wu, wd, nfin)


def _conv_post(y, lg, lb, gc):
    mu = jnp.mean(y, axis=-1, keepdims=True)
    yc = y - mu
    var = jnp.mean(yc * yc, axis=-1, keepdims=True)
    yl = yc * lax.rsqrt(var + EPS) * lg + lb
    ys = yl * _sigmoid(yl)
    return ys * _rms_scale(ys) * gc


def _b_projection(sb, bre_ref, bim_ref, xr, xi, o):
    per = STATES_PER_OUT_TILE // STATE_TILE
    for i in range(per):
        q = per * o + i
        first_channel = q * STATE_TILE // SSM_STATE * SSM_GROUP_CH
        ks = first_channel // B_KSLICE * B_KSLICE
        lhs = sb[:, ks:ks + B_KSLICE]
        xr[:, STATE_TILE * i:STATE_TILE * (i + 1)] = jnp.dot(
            lhs, bre_ref[q], preferred_element_type=F32)
        xi[:, STATE_TILE * i:STATE_TILE * (i + 1)] = jnp.dot(
            lhs, bim_ref[q], preferred_element_type=F32)


def _c_projection(xr, xi, cc_ref, o):
    hcat = jnp.concatenate([xr[...].astype(BF16), xi[...].astype(BF16)], axis=1)
    return jnp.dot(hcat, cc_ref[o], preferred_element_type=F32)


def _ssm_post(yss, s, d, wglu_ref, gs):
    y = _gelu_tanh(yss + d * s)
    z = jnp.dot(y.astype(BF16), wglu_ref[...], preferred_element_type=F32)
    y = y * _sigmoid(z)
    return y * _rms_scale(y) * gs


def _mixer_prompt_kernel(v_ref, gt_ref, s_ref, cw_ref, cb_ref, lg_ref, lb_ref, gc_ref,
                         bre_ref, bim_ref, cc_ref, tab_ref, d_ref, wglu_ref, gs_ref,
                         mix_ref, cst_ref, hre_ref, him_ref,
                         ush, xr, xi, hsr, hsi, yss, *, tc):
    c = pl.program_id(1)
    pad = CONV_PAD
    ubuf = ush.at[0]

    @pl.when(c == 0)
    def _():
        ubuf[0:pad, :] = jnp.zeros((pad, D_CONV), F32)
        ubuf[pad + tc:pad + tc + SUBLANES, :] = jnp.zeros((SUBLANES, D_CONV), F32)
        hsr[...] = jnp.zeros_like(hsr)
        hsi[...] = jnp.zeros_like(hsi)

    ubuf[pad:pad + tc, :] = v_ref[...] * _sigmoid(gt_ref[...])

    row_id = lax.broadcasted_iota(jnp.int32, (SUBLANES, D_CONV), 0)

    def shift_body(j, carry):
        r0 = pl.multiple_of(j * SUBLANES, SUBLANES)
        lo = ubuf[pl.ds(r0, SUBLANES), :]
        hi = ubuf[pl.ds(r0 + SUBLANES, SUBLANES), :]
        for b in range(1, SUBLANES):
            ush[b, pl.ds(r0, SUBLANES), :] = pltpu.roll(
                jnp.where(row_id >= b, lo, hi), SUBLANES - b, 0)
        return carry

    lax.fori_loop(0, (pad + tc) // SUBLANES, shift_body, 0)

    rows = 16
    off = pad - HIST

    def conv_body(m, carry):
        r0 = pl.multiple_of(m * rows, rows)
        acc = jnp.zeros((rows, D_CONV), F32) + cb_ref[...]
        for k in range(CONV_WIDTH):
            a, b = divmod(k + off, SUBLANES)
            acc = acc + cw_ref[k:k + 1, :] * ush[b, pl.ds(r0 + SUBLANES * a, rows), :]
        ycn = _conv_post(acc, lg_ref[...], lb_ref[...], gc_ref[...])
        mix_ref[pl.ds(r0, rows), 0:D_CONV] = ycn.astype(BF16)
        return carry

    lax.fori_loop(0, tc // rows, conv_body, 0)

    @pl.when(c == pl.num_programs(1) - 1)
    def _():
        cst_ref[0] = ubuf[pad + tc - HIST:pad + tc, :]

    ubuf[0:pad, :] = ubuf[tc:tc + pad, :]

    s = s_ref[...]
    sb = s.astype(BF16)
    for o in range(N_OUT_TILES):
        _b_projection(sb, bre_ref, bim_ref, xr, xi, o)
        lanes = slice(STATES_PER_OUT_TILE * o, STATES_PER_OUT_TILE * (o + 1))

        def scan_body(k, carry):
            cr, ci = carry
            r0 = pl.multiple_of(k * SUBLANES, SUBLANES)
            tr = xr[pl.ds(r0, SUBLANES), :]
            ti = xi[pl.ds(r0, SUBLANES), :]
            for shift, ia, ib in ((1, T_A1R, T_A1I), (2, T_A2R, T_A2I), (4, T_A4R, T_A4I)):
                ar = tab_ref[ia, :, lanes]
                ai = tab_ref[ib, :, lanes]
                pr = pltpu.roll(tr, shift, 0)
                pi = pltpu.roll(ti, shift, 0)
                tr, ti = tr + ar * pr - ai * pi, ti + ar * pi + ai * pr
            pwr = tab_ref[T_PWR, :, lanes]
            pwi = tab_ref[T_PWI, :, lanes]
            tr, ti = tr + pwr * cr - pwi * ci, ti + pwr * ci + pwi * cr
            xr[pl.ds(r0, SUBLANES), :] = tr
            xi[pl.ds(r0, SUBLANES), :] = ti
            ncr = jnp.broadcast_to(tr[SUBLANES - 1:SUBLANES, :], tr.shape)
            nci = jnp.broadcast_to(ti[SUBLANES - 1:SUBLANES, :], ti.shape)
            return ncr, nci

        cr, ci = lax.fori_loop(0, tc // SUBLANES, scan_body, (hsr[:, lanes], hsi[:, lanes]))
        hsr[:, lanes] = cr
        hsi[:, lanes] = ci
        yss[:, OUT_TILE * o:OUT_TILE * (o + 1)] = _c_projection(xr, xi, cc_ref, o)

    ysn = _ssm_post(yss[...], s, d_ref[...], wglu_ref, gs_ref[...])
    mix_ref[:, D_CONV:D_CONV + D_SSM] = ysn.astype(BF16)

    @pl.when(c == pl.num_programs(1) - 1)
    def _():
        hre_ref[0] = hsr[0:1, :]
        him_ref[0] = hsi[0:1, :]


def _mixer_prompt(proj, batch, seq, cw, cb, lg, lb, gc, bre, bim, cc, tab, dvec, wglu, gs, tc):
    nc = seq // tc
    const2 = lambda b, c: (0, 0)
    const3 = lambda b, c: (0, 0, 0)
    row = lambda col: (lambda b, c: (b * nc + c, col))
    vec = pl.BlockSpec((1, D_CONV), const2)
    return pl.pallas_call(
        functools.partial(_mixer_prompt_kernel, tc=tc),
        out_shape=(
            jax.ShapeDtypeStruct((batch * seq, D_MODEL), BF16),
            jax.ShapeDtypeStruct((batch, HIST, D_CONV), F32),
            jax.ShapeDtypeStruct((batch, 1, N_STATE), F32),
            jax.ShapeDtypeStruct((batch, 1, N_STATE), F32),
        ),
        grid=(batch, nc),
        in_specs=[
            pl.BlockSpec((tc, D_CONV), row(0)),
            pl.BlockSpec((tc, D_CONV), row(1)),
            pl.BlockSpec((tc, D_SSM), row(2)),
            pl.BlockSpec((CONV_WIDTH, D_CONV), const2),
            vec, vec, vec, vec,
            pl.BlockSpec(bre.shape, const3),
            pl.BlockSpec(bim.shape, const3),
            pl.BlockSpec(cc.shape, const3),
            pl.BlockSpec(tab.shape, const3),
            vec,
            pl.BlockSpec((D_SSM, D_SSM), const2),
            vec,
        ],
        out_specs=(
            pl.BlockSpec((tc, D_MODEL), lambda b, c: (b * nc + c, 0)),
            pl.BlockSpec((1, HIST, D_CONV), lambda b, c: (b, 0, 0)),
            pl.BlockSpec((1, 1, N_STATE), lambda b, c: (b, 0, 0)),
            pl.BlockSpec((1, 1, N_STATE), lambda b, c: (b, 0, 0)),
        ),
        scratch_shapes=[
            pltpu.VMEM((SUBLANES, CONV_PAD + tc + SUBLANES, D_CONV), F32),
            pltpu.VMEM((tc, STATES_PER_OUT_TILE), F32),
            pltpu.VMEM((tc, STATES_PER_OUT_TILE), F32),
            pltpu.VMEM((SUBLANES, N_STATE), F32),
            pltpu.VMEM((SUBLANES, N_STATE), F32),
            pltpu.VMEM((tc, D_SSM), F32),
        ],
        compiler_params=pltpu.CompilerParams(
            dimension_semantics=("arbitrary", "arbitrary"), vmem_limit_bytes=VMEM_LIMIT),
        name="mixer_prompt",
    )(proj, proj, proj, cw, cb, lg, lb, gc, bre, bim, cc, tab, dvec, wglu, gs)


def _mixer_sample_kernel(v_ref, gt_ref, s_ref, sct_ref, h0r_ref, h0i_ref,
                         cw_ref, cb_ref, lg_ref, lb_ref, gc_ref,
                         bre_ref, bim_ref, cc_ref, tab_ref, d_ref, wglu_ref, gs_ref,
                         mix_ref, nct_ref, hre_ref, him_ref,
                         xr, xi, yss, *, steps, tb):
    rows = steps * tb
    u = (v_ref[...] * _sigmoid(gt_ref[...]))

    def window(j):
        return sct_ref[j] if j < HIST else u[j - HIST]

    for t in range(steps):
        acc = jnp.zeros((tb, D_CONV), F32) + cb_ref[...]
        for k in range(CONV_WIDTH):
            acc = acc + cw_ref[k:k + 1, :] * window(t + k)
        ycn = _conv_post(acc, lg_ref[...], lb_ref[...], gc_ref[...])
        mix_ref[t, :, 0:D_CONV] = ycn.astype(BF16)
    for j in range(HIST):
        nct_ref[j] = window(steps + j)

    s = s_ref[...].reshape(rows, D_SSM)
    sb = s.astype(BF16)
    for o in range(N_OUT_TILES):
        _b_projection(sb, bre_ref, bim_ref, xr, xi, o)
        lanes = slice(STATES_PER_OUT_TILE * o, STATES_PER_OUT_TILE * (o + 1))
        ar = tab_ref[T_AR, 0:1, lanes]
        ai = tab_ref[T_AI, 0:1, lanes]
        hr = h0r_ref[:, lanes]
        hi = h0i_ref[:, lanes]
        for t in range(steps):
            rs = slice(t * tb, (t + 1) * tb)
            hr, hi = ar * hr - ai * hi + xr[rs, :], ar * hi + ai * hr + xi[rs, :]
            xr[rs, :] = hr
            xi[rs, :] = hi
        hre_ref[:, lanes] = hr
        him_ref[:, lanes] = hi
        yss[:, OUT_TILE * o:OUT_TILE * (o + 1)] = _c_projection(xr, xi, cc_ref, o)

    ysn = _ssm_post(yss[...], s, d_ref[...], wglu_ref, gs_ref[...])
    mix_ref[:, :, D_CONV:D_CONV + D_SSM] = ysn.astype(BF16).reshape(steps, tb, D_SSM)


def _mixer_sample(proj, sct, h0r, h0i, cw, cb, lg, lb, gc, bre, bim, cc, tab, dvec, wglu, gs, tb):
    steps, batch, _ = proj.shape
    const2 = lambda i: (0, 0)
    const3 = lambda i: (0, 0, 0)
    vec = pl.BlockSpec((1, D_CONV), const2)
    col = lambda c: (lambda i: (0, i, c))
    st = pl.BlockSpec((tb, N_STATE), lambda i: (i, 0))
    return pl.pallas_call(
        functools.partial(_mixer_sample_kernel, steps=steps, tb=tb),
        out_shape=(
            jax.ShapeDtypeStruct((steps, batch, D_MODEL), BF16),
            jax.ShapeDtypeStruct((HIST, batch, D_CONV), F32),
            jax.ShapeDtypeStruct((batch, N_STATE), F32),
            jax.ShapeDtypeStruct((batch, N_STATE), F32),
        ),
        grid=(batch // tb,),
        in_specs=[
            pl.BlockSpec((steps, tb, D_CONV), col(0)),
            pl.BlockSpec((steps, tb, D_CONV), col(1)),
            pl.BlockSpec((steps, tb, D_SSM), col(2)),
            pl.BlockSpec((HIST, tb, D_CONV), lambda i: (0, i, 0)),
            st, st,
            pl.BlockSpec((CONV_WIDTH, D_CONV), const2),
            vec, vec, vec, vec,
            pl.BlockSpec(bre.shape, const3),
            pl.BlockSpec(bim.shape, const3),
            pl.BlockSpec(cc.shape, const3),
            pl.BlockSpec(tab.shape, const3),
            vec,
            pl.BlockSpec((D_SSM, D_SSM), const2),
            vec,
        ],
        out_specs=(
            pl.BlockSpec((steps, tb, D_MODEL), lambda i: (0, i, 0)),
            pl.BlockSpec((HIST, tb, D_CONV), lambda i: (0, i, 0)),
            st, st,
        ),
        scratch_shapes=[
            pltpu.VMEM((steps * tb, STATES_PER_OUT_TILE), F32),
            pltpu.VMEM((steps * tb, STATES_PER_OUT_TILE), F32),
            pltpu.VMEM((steps * tb, D_SSM), F32),
        ],
        compiler_params=pltpu.CompilerParams(
            dimension_semantics=("arbitrary",), vmem_limit_bytes=VMEM_LIMIT),
        name="mixer_sample",
    )(proj, proj, proj, sct, h0r, h0i, cw, cb, lg, lb, gc, bre, bim, cc, tab, dvec, wglu, gs)


def _ssm_tables(a_re, a_im, log_dt, b_re, b_im, c_re, c_im):
    ar = a_re.astype(F32)
    ai = a_im.astype(F32)
    dt = jnp.exp(log_dt.astype(F32))[:, None]

    def power(n):
        mag = jnp.exp(n * dt * ar)
        return (mag * jnp.cos(n * dt * ai)).reshape(N_STATE), (mag * jnp.sin(n * dt * ai)).reshape(N_STATE)

    mag = jnp.exp(dt * ar)
    ab_re = mag * jnp.cos(dt * ai)
    ab_im = mag * jnp.sin(dt * ai)
    den = ar * ar + ai * ai
    nr = ab_re - 1.0
    ni = ab_im
    k_re = (nr * ar + ni * ai) / den
    k_im = (ni * ar - nr * ai) / den
    br = b_re.astype(F32)
    bi = b_im.astype(F32)
    bb_re = k_re[..., None] * br - k_im[..., None] * bi
    bb_im = k_re[..., None] * bi + k_im[..., None] * br

    row = jnp.arange(SUBLANES)[:, None]
    tabs = []
    for shift in (1, 2, 4):
        pr, pi = power(float(shift))
        keep = row >= shift
        tabs += [jnp.where(keep, pr[None, :], 0.0), jnp.where(keep, pi[None, :], 0.0)]
    pws = [power(float(i + 1)) for i in range(SUBLANES)]
    tabs += [jnp.stack([p[0] for p in pws]), jnp.stack([p[1] for p in pws])]
    tabs += [jnp.broadcast_to(ab_re.reshape(1, N_STATE), (SUBLANES, N_STATE)),
             jnp.broadcast_to(ab_im.reshape(1, N_STATE), (SUBLANES, N_STATE))]
    tab = jnp.stack(tabs).astype(F32)

    groups_per_tile = STATE_TILE // SSM_STATE
    eye = jnp.eye(groups_per_tile, dtype=F32)

    def b_tiles(bb):
        bt = jnp.transpose(bb, (0, 2, 1)).reshape(N_STATE_TILES, groups_per_tile, SSM_GROUP_CH, SSM_STATE)
        blk = bt[:, :, :, None, :] * eye[None, :, None, :, None]
        blk = blk.reshape(N_STATE_TILES, groups_per_tile * SSM_GROUP_CH, STATE_TILE)
        halves = B_KSLICE // (groups_per_tile * SSM_GROUP_CH)
        sel = (jnp.arange(N_STATE_TILES)[:, None] % halves) == jnp.arange(halves)[None, :]
        out = blk[:, None, :, :] * sel[:, :, None, None].astype(F32)
        return out.reshape(N_STATE_TILES, B_KSLICE, STATE_TILE).astype(BF16)

    groups_per_out = OUT_TILE // SSM_GROUP_CH
    eye_o = jnp.eye(groups_per_out, dtype=F32)

    def c_rows(cm):
        ct = jnp.transpose(cm.astype(F32), (0, 2, 1)).reshape(N_OUT_TILES, groups_per_out, SSM_STATE, SSM_GROUP_CH)
        blk = ct[:, :, :, None, :] * eye_o[None, :, None, :, None]
        return blk.reshape(N_OUT_TILES, STATES_PER_OUT_TILE, OUT_TILE)

    cc = jnp.concatenate([c_rows(c_re), -c_rows(c_im)], axis=1).astype(BF16)
    return tab, b_tiles(bb_re), b_tiles(bb_im), cc


def kernel(x_prompt, x_sample, state_conv, state_ssm_re, state_ssm_im, norm_mix, w_in, conv_w, conv_b, conv_ln_g, conv_ln_b, ssm_A_re, ssm_A_im, ssm_log_dt, ssm_B_re, ssm_B_im, ssm_C_re, ssm_C_im, ssm_D, w_glu, gnorm_conv, gnorm_ssm, w_out, norm_ffn, w_ffn_gate, w_ffn_up, w_ffn_down, norm_final):
    depth = norm_mix.shape[0]
    batch, seq, _ = x_prompt.shape
    dbatch, dsteps, _ = x_sample.shape

    hp = x_prompt.reshape(batch * seq, D_MODEL)
    hs = jnp.transpose(x_sample, (1, 0, 2)).reshape(dsteps * dbatch, D_MODEL)

    pc, pr, pi_, sc, sr, si = [], [], [], [], [], []
    for d in range(depth):
        tab, bre, bim, cc = _ssm_tables(ssm_A_re[d], ssm_A_im[d], ssm_log_dt[d],
                                        ssm_B_re[d], ssm_B_im[d], ssm_C_re[d], ssm_C_im[d])
        vec = lambda a: a[d].reshape(1, -1).astype(F32)
        w_in_b = w_in[d].astype(BF16)
        wglu_b = w_glu[d].astype(BF16)
        wo_b = w_out[d].astype(BF16)
        wg_b = w_ffn_gate[d].astype(BF16)
        wu_b = w_ffn_up[d].astype(BF16)
        wd_b = w_ffn_down[d].astype(BF16)
        cw = conv_w[d].astype(F32)
        mixer_params = (cw, vec(conv_b), vec(conv_ln_g), vec(conv_ln_b), vec(gnorm_conv),
                        bre, bim, cc, tab, vec(ssm_D), wglu_b, vec(gnorm_ssm))
        last = d == depth - 1
        nfin = norm_final.reshape(1, -1).astype(F32)

        proj = _in_proj(hp, vec(norm_mix), w_in_b, tm=512)
        mix, cst, hre, him = _mixer_prompt(proj, batch, seq, *mixer_params, tc=256)
        hp = _ffn(mix, hp, wo_b, vec(norm_ffn), wg_b, wu_b, wd_b, nfin, tm=512, final_norm=last)
        pc.append(cst)
        pr.append(hre.reshape(batch, N_GROUPS, SSM_STATE))
        pi_.append(him.reshape(batch, N_GROUPS, SSM_STATE))

        proj = _in_proj(hs, vec(norm_mix), w_in_b, tm=512).reshape(dsteps, dbatch, D_IN)
        sct = jnp.transpose(state_conv[d], (1, 0, 2))
        h0r = state_ssm_re[d].reshape(dbatch, N_STATE)
        h0i = state_ssm_im[d].reshape(dbatch, N_STATE)
        mix, nct, hre, him = _mixer_sample(proj, sct, h0r, h0i, *mixer_params, tb=32)
        hs = _ffn(mix.reshape(dsteps * dbatch, D_MODEL), hs, wo_b, vec(norm_ffn), wg_b, wu_b, wd_b, nfin,
                  tm=512, final_norm=last)
        sc.append(jnp.transpose(nct, (1, 0, 2)))
        sr.append(hre.reshape(dbatch, N_GROUPS, SSM_STATE))
        si.append(him.reshape(dbatch, N_GROUPS, SSM_STATE))

    y_prompt = hp.reshape(batch, seq, D_MODEL)
    y_sample = jnp.transpose(hs.reshape(dsteps, dbatch, D_MODEL), (1, 0, 2))
    return (y_prompt, y_sample, jnp.stack(pc), jnp.stack(pr), jnp.stack(pi_),
            jnp.stack(sc), jnp.stack(sr), jnp.stack(si))
```

```python
import functools
import math

import jax
import jax.numpy as jnp
from jax import lax
from jax.experimental import pallas as pl
from jax.experimental.pallas import tpu as pltpu

F32 = jnp.float32
BF16 = jnp.bfloat16

D_MODEL = 2048
D_CONV = 1024
D_SSM = 1024
D_IN = 2 * D_CONV + D_SSM
CONV_WIDTH = 31
HIST = CONV_WIDTH - 1
SSM_GROUP_CH = 16
N_GROUPS = D_SSM // SSM_GROUP_CH
SSM_STATE = 64
N_STATE = N_GROUPS * SSM_STATE
D_FF = 5632
EPS = 1e-6

SUBLANES = 8
CONV_PAD = 32
STATE_TILE = 256
N_STATE_TILES = N_STATE // STATE_TILE
B_KSLICE = 128
OUT_TILE = 256
N_OUT_TILES = D_SSM // OUT_TILE
STATES_PER_OUT_TILE = N_STATE // N_OUT_TILES
VMEM_LIMIT = 56 * 1024 * 1024

(T_B1R, T_B1I, T_B2R, T_B2I, T_B4R, T_B4I,
 T_PSR, T_PSI,
 T_PNR, T_PNI,
 T_AR, T_AI) = range(12)


def _sigmoid(x):
    return 1.0 / (1.0 + jnp.exp(-x))


def _rms_scale(x):
    return lax.rsqrt(jnp.mean(x * x, axis=-1, keepdims=True) + EPS)


def _gelu_tanh(x):
    c = math.sqrt(2.0 / math.pi)
    return x * (0.5 * (1.0 + jnp.tanh(c * (x + 0.044715 * (x * x * x)))))


def _cmul(ar, ai, xr, xi):
    return ar * xr - ai * xi, ar * xi + ai * xr


def _in_proj_kernel(x_ref, g_ref, w_ref, o_ref, xn_ref):
    @pl.when(pl.program_id(1) == 0)
    def _():
        x = x_ref[...]
        xn_ref[...] = (x * _rms_scale(x) * g_ref[...]).astype(BF16)

    o_ref[...] = jnp.dot(xn_ref[...], w_ref[...], preferred_element_type=F32)


def _in_proj(x, g, w, tm):
    rows = x.shape[0]
    tn = 1024
    return pl.pallas_call(
        _in_proj_kernel,
        out_shape=jax.ShapeDtypeStruct((rows, D_IN), F32),
        grid=(rows // tm, D_IN // tn),
        in_specs=[
            pl.BlockSpec((tm, D_MODEL), lambda i, j: (i, 0)),
            pl.BlockSpec((1, D_MODEL), lambda i, j: (0, 0)),
            pl.BlockSpec((D_MODEL, tn), lambda i, j: (0, j)),
        ],
        out_specs=pl.BlockSpec((tm, tn), lambda i, j: (i, j)),
        scratch_shapes=[pltpu.VMEM((tm, D_MODEL), BF16)],
        compiler_params=pltpu.CompilerParams(
            dimension_semantics=("arbitrary", "arbitrary"), vmem_limit_bytes=VMEM_LIMIT),
        name="in_proj",
    )(x, g, w)


def _ffn_kernel(mix_ref, x_ref, wo_ref, nf_ref, wg_ref, wu_ref, wd_ref, nfin_ref, o_ref, hn_ref,
                *, final_norm):
    f = pl.program_id(1)

    @pl.when(f == 0)
    def _():
        h = x_ref[...] + jnp.dot(mix_ref[...], wo_ref[...], preferred_element_type=F32)
        o_ref[...] = h
        hn_ref[...] = (h * _rms_scale(h) * nf_ref[...]).astype(BF16)

    hn = hn_ref[...]
    gate = jnp.dot(hn, wg_ref[...], preferred_element_type=F32)
    up = jnp.dot(hn, wu_ref[...], preferred_element_type=F32)
    act = (gate * _sigmoid(gate)) * up
    o_ref[...] += jnp.dot(act.astype(BF16), wd_ref[...], preferred_element_type=F32)

    if final_norm:
        @pl.when(f == pl.num_programs(1) - 1)
        def _():
            h = o_ref[...]
            o_ref[...] = h * _rms_scale(h) * nfin_ref[...]


def _ffn(mix, x, wo, nf, wg, wu, wd, nfin, tm, final_norm):
    rows = x.shape[0]
    tf = 512
    return pl.pallas_call(
        functools.partial(_ffn_kernel, final_norm=final_norm),
        out_shape=jax.ShapeDtypeStruct((rows, D_MODEL), F32),
        grid=(rows // tm, D_FF // tf),
        in_specs=[
            pl.BlockSpec((tm, D_MODEL), lambda i, f: (i, 0)),
            pl.BlockSpec((tm, D_MODEL), lambda i, f: (i, 0)),
            pl.BlockSpec((D_MODEL, D_MODEL), lambda i, f: (0, 0)),
            pl.BlockSpec((1, D_MODEL), lambda i, f: (0, 0)),
            pl.BlockSpec((D_MODEL, tf), lambda i, f: (0, f)),
            pl.BlockSpec((D_MODEL, tf), lambda i, f: (0, f)),
            pl.BlockSpec((tf, D_MODEL), lambda i, f: (f, 0)),
            pl.BlockSpec((1, D_MODEL), lambda i, f: (0, 0)),
        ],
        out_specs=pl.BlockSpec((tm, D_MODEL), lambda i, f: (i, 0)),
        scratch_shapes=[pltpu.VMEM((tm, D_MODEL), BF16)],
        compiler_params=pltpu.CompilerParams(
            dimension_semantics=("arbitrary", "arbitrary"), vmem_limit_bytes=VMEM_LIMIT),
        name="ffn",
    )(mix, x, wo, nf, wg, wu, wd, nfin)


def _conv_post(y, lg, lb, gc):
    mu = jnp.mean(y, axis=-1, keepdims=True)
    yc = y - mu
    var = jnp.mean(yc * yc, axis=-1, keepdims=True)
    yl = yc * lax.rsqrt(var + EPS) * lg + lb
    ys = yl * _sigmoid(yl)
    return ys * _rms_scale(ys) * gc


def _b_projection(sb, bre_ref, bim_ref, xr, xi, o):
    per = STATES_PER_OUT_TILE // STATE_TILE
    for i in range(per):
        q = per * o + i
        first_channel = q * STATE_TILE // SSM_STATE * SSM_GROUP_CH
        ks = first_channel // B_KSLICE * B_KSLICE
        lhs = sb[:, ks:ks + B_KSLICE]
        xr[:, STATE_TILE * i:STATE_TILE * (i + 1)] = jnp.dot(
            lhs, bre_ref[q], preferred_element_type=F32)
        xi[:, STATE_TILE * i:STATE_TILE * (i + 1)] = jnp.dot(
            lhs, bim_ref[q], preferred_element_type=F32)


def _c_projection(xr, xi, cc_ref, o):
    hcat = jnp.concatenate([xr[...].astype(BF16), xi[...].astype(BF16)], axis=1)
    return jnp.dot(hcat, cc_ref[o], preferred_element_type=F32)


def _ssm_post(yss, s, d, wglu_ref, gs):
    y = _gelu_tanh(yss + d * s)
    z = jnp.dot(y.astype(BF16), wglu_ref[...], preferred_element_type=F32)
    y = y * _sigmoid(z)
    return y * _rms_scale(y) * gs


def _split_bf16(x):
    hi = x.astype(BF16)
    r = x - hi.astype(F32)
    mid = r.astype(BF16)
    lo = (r - mid.astype(F32)).astype(BF16)
    return hi, mid, lo


def _mixer_prompt_kernel(v_ref, gt_ref, s_ref, cwb_ref, cb_ref, lg_ref, lb_ref, gc_ref,
                         bre_ref, bim_ref, cc_ref, tab_ref, d_ref, wglu_ref, gs_ref,
                         perm_ref, permt_ref,
                         mix_ref, cst_ref, hre_ref, him_ref,
                         ush, ycv, xr, xi, hsr, hsi, yss, *, tc):
    c = pl.program_id(1)
    pad = CONV_PAD
    ubuf = ush.at[0]

    @pl.when(c == 0)
    def _():
        ubuf[0:pad, :] = jnp.zeros((pad, D_CONV), F32)
        ubuf[pad + tc:pad + tc + SUBLANES, :] = jnp.zeros((SUBLANES, D_CONV), F32)
        hsr[...] = jnp.zeros_like(hsr)
        hsi[...] = jnp.zeros_like(hsi)

    ubuf[pad:pad + tc, :] = v_ref[...] * _sigmoid(gt_ref[...])

    row_id = lax.broadcasted_iota(jnp.int32, (SUBLANES, D_CONV), 0)

    def shift_body(j, carry):
        r0 = pl.multiple_of(j * SUBLANES, SUBLANES)
        lo = ubuf[pl.ds(r0, SUBLANES), :]
        hi = ubuf[pl.ds(r0 + SUBLANES, SUBLANES), :]
        for b in range(1, SUBLANES):
            ush[b, pl.ds(r0, SUBLANES), :] = pltpu.roll(
                jnp.where(row_id >= b, lo, hi), SUBLANES - b, 0)
        return carry

    lax.fori_loop(0, (pad + tc) // SUBLANES, shift_body, 0)

    rows = 32
    lw = 512
    off = pad - HIST

    taps_by_shift = [[(k, (k + off) // SUBLANES) for k in range(CONV_WIDTH) if (k + off) % SUBLANES == b]
                     for b in range(SUBLANES)]
    out_tiles = rows // SUBLANES

    def conv_body(m, carry):
        r0 = pl.multiple_of(m * rows, rows)
        for lh in range(D_CONV // lw):
            ls = slice(lh * lw, (lh + 1) * lw)
            acc = [jnp.zeros((SUBLANES, lw), F32) for _ in range(out_tiles)]
            for b, taps in enumerate(taps_by_shift):
                first = min(a for _, a in taps)
                last = max(a for _, a in taps) + out_tiles
                tiles = {t: ush[b, pl.ds(r0 + SUBLANES * t, SUBLANES), ls] for t in range(first, last)}
                for k, a in taps:
                    w = cwb_ref[k, :, ls]
                    for i in range(out_tiles):
                        acc[i] = acc[i] + tiles[a + i] * w
            for i in range(out_tiles):
                ycv[pl.ds(r0 + SUBLANES * i, SUBLANES), ls] = acc[i]
        return carry

    lax.fori_loop(0, tc // rows, conv_body, 0)
    ycn = _conv_post(ycv[...] + cb_ref[...], lg_ref[...], lb_ref[...], gc_ref[...])
    mix_ref[:, 0:D_CONV] = ycn.astype(BF16)

    @pl.when(c == pl.num_programs(1) - 1)
    def _():
        cst_ref[0] = ubuf[pad + tc - HIST:pad + tc, :]

    ubuf[0:pad, :] = ubuf[tc:tc + pad, :]

    seg = tc // SUBLANES
    perm = perm_ref[...]
    s1, s2, s3 = _split_bf16(s_ref[...])
    sp1 = jnp.dot(perm, s1, preferred_element_type=F32)
    sp = sp1 + jnp.dot(perm, s2, preferred_element_type=F32) + jnp.dot(perm, s3, preferred_element_type=F32)
    sb = sp1.astype(BF16)
    keep1 = lax.broadcasted_iota(jnp.int32, (SUBLANES, STATES_PER_OUT_TILE), 0) >= 1

    for o in range(N_OUT_TILES):
        _b_projection(sb, bre_ref, bim_ref, xr, xi, o)
        lanes = slice(STATES_PER_OUT_TILE * o, STATES_PER_OUT_TILE * (o + 1))
        ar = tab_ref[T_AR, :, lanes]
        ai = tab_ref[T_AI, :, lanes]

        def step(j, h, store):
            r0 = pl.multiple_of(j * SUBLANES, SUBLANES)
            pr, pi = _cmul(ar, ai, h[0], h[1])
            hr = pr + xr[pl.ds(r0, SUBLANES), :]
            hi = pi + xi[pl.ds(r0, SUBLANES), :]
            if store:
                xr[pl.ds(r0, SUBLANES), :] = hr
                xi[pl.ds(r0, SUBLANES), :] = hi
            return hr, hi

        zero = jnp.zeros((SUBLANES, STATES_PER_OUT_TILE), F32)
        fr, fi = lax.fori_loop(0, seg, functools.partial(step, store=False), (zero, zero), unroll=2)

        for shift, ia, ib in ((1, T_B1R, T_B1I), (2, T_B2R, T_B2I), (4, T_B4R, T_B4I)):
            pr, pi = _cmul(tab_ref[ia, :, lanes], tab_ref[ib, :, lanes],
                           pltpu.roll(fr, shift, 0), pltpu.roll(fi, shift, 0))
            fr, fi = fr + pr, fi + pi
        cr = hsr[:, lanes]
        ci = hsi[:, lanes]
        pr, pi = _cmul(tab_ref[T_PSR, :, lanes], tab_ref[T_PSI, :, lanes], cr, ci)
        init_r = pr + jnp.where(keep1, pltpu.roll(fr, 1, 0), 0.0)
        init_i = pi + jnp.where(keep1, pltpu.roll(fi, 1, 0), 0.0)
        pr, pi = _cmul(tab_ref[T_PNR, :, lanes], tab_ref[T_PNI, :, lanes], cr, ci)
        hsr[:, lanes] = jnp.broadcast_to((pr + fr)[SUBLANES - 1:SUBLANES, :], cr.shape)
        hsi[:, lanes] = jnp.broadcast_to((pi + fi)[SUBLANES - 1:SUBLANES, :], ci.shape)

        lax.fori_loop(0, seg, functools.partial(step, store=True), (init_r, init_i), unroll=2)
        yss[:, OUT_TILE * o:OUT_TILE * (o + 1)] = _c_projection(xr, xi, cc_ref, o)

    ysn = _ssm_post(yss[...], sp, d_ref[...], wglu_ref, gs_ref[...])
    mix_ref[:, D_CONV:D_CONV + D_SSM] = jnp.dot(
        permt_ref[...], ysn.astype(BF16), preferred_element_type=F32).astype(BF16)

    @pl.when(c == pl.num_programs(1) - 1)
    def _():
        hre_ref[0] = hsr[0:1, :]
        him_ref[0] = hsi[0:1, :]


def _mixer_prompt(proj, batch, seq, cwb, cb, lg, lb, gc, bre, bim, cc, tab, dvec, wglu, gs, perm, tc):
    nc = seq // tc
    const2 = lambda b, c: (0, 0)
    const3 = lambda b, c: (0, 0, 0)
    row = lambda col: (lambda b, c: (b * nc + c, col))
    vec = pl.BlockSpec((1, D_CONV), const2)
    return pl.pallas_call(
        functools.partial(_mixer_prompt_kernel, tc=tc),
        out_shape=(
            jax.ShapeDtypeStruct((batch * seq, D_MODEL), BF16),
            jax.ShapeDtypeStruct((batch, HIST, D_CONV), F32),
            jax.ShapeDtypeStruct((batch, 1, N_STATE), F32),
            jax.ShapeDtypeStruct((batch, 1, N_STATE), F32),
        ),
        grid=(batch, nc),
        in_specs=[
            pl.BlockSpec((tc, D_CONV), row(0)),
            pl.BlockSpec((tc, D_CONV), row(1)),
            pl.BlockSpec((tc, D_SSM), row(2)),
            pl.BlockSpec(cwb.shape, const3),
            vec, vec, vec, vec,
            pl.BlockSpec(bre.shape, const3),
            pl.BlockSpec(bim.shape, const3),
            pl.BlockSpec(cc.shape, const3),
            pl.BlockSpec(tab.shape, const3),
            vec,
            pl.BlockSpec((D_SSM, D_SSM), const2),
            vec,
            pl.BlockSpec((tc, tc), const2),
            pl.BlockSpec((tc, tc), const2),
        ],
        out_specs=(
            pl.BlockSpec((tc, D_MODEL), lambda b, c: (b * nc + c, 0)),
            pl.BlockSpec((1, HIST, D_CONV), lambda b, c: (b, 0, 0)),
            pl.BlockSpec((1, 1, N_STATE), lambda b, c: (b, 0, 0)),
            pl.BlockSpec((1, 1, N_STATE), lambda b, c: (b, 0, 0)),
        ),
        scratch_shapes=[
            pltpu.VMEM((SUBLANES, CONV_PAD + tc + SUBLANES, D_CONV), F32),
            pltpu.VMEM((tc, D_CONV), F32),
            pltpu.VMEM((tc, STATES_PER_OUT_TILE), F32),
            pltpu.VMEM((tc, STATES_PER_OUT_TILE), F32),
            pltpu.VMEM((SUBLANES, N_STATE), F32),
            pltpu.VMEM((SUBLANES, N_STATE), F32),
            pltpu.VMEM((tc, D_SSM), F32),
        ],
        compiler_params=pltpu.CompilerParams(
            dimension_semantics=("arbitrary", "arbitrary"), vmem_limit_bytes=VMEM_LIMIT),
        name="mixer_prompt",
    )(proj, proj, proj, cwb, cb, lg, lb, gc, bre, bim, cc, tab, dvec, wglu, gs, perm, perm.T)


def _mixer_sample_kernel(v_ref, gt_ref, s_ref, sct_ref, h0r_ref, h0i_ref,
                         cwb_ref, cb_ref, lg_ref, lb_ref, gc_ref,
                         bre_ref, bim_ref, cc_ref, tab_ref, d_ref, wglu_ref, gs_ref,
                         mix_ref, nct_ref, hre_ref, him_ref,
                         xr, xi, yss, *, steps, tb):
    rows = steps * tb
    u = (v_ref[...] * _sigmoid(gt_ref[...]))

    def window(j):
        return sct_ref[j] if j < HIST else u[j - HIST]

    for t in range(steps):
        acc = jnp.zeros((tb // SUBLANES, SUBLANES, D_CONV), F32)
        for k in range(CONV_WIDTH):
            acc = acc + window(t + k).reshape(tb // SUBLANES, SUBLANES, D_CONV) * cwb_ref[k][None]
        ycn = _conv_post(acc.reshape(tb, D_CONV) + cb_ref[...], lg_ref[...], lb_ref[...], gc_ref[...])
        mix_ref[t, :, 0:D_CONV] = ycn.astype(BF16)
    for j in range(HIST):
        nct_ref[j] = window(steps + j)

    s = s_ref[...].reshape(rows, D_SSM)
    sb = s.astype(BF16)
    for o in range(N_OUT_TILES):
        _b_projection(sb, bre_ref, bim_ref, xr, xi, o)
        lanes = slice(STATES_PER_OUT_TILE * o, STATES_PER_OUT_TILE * (o + 1))
        ar = tab_ref[T_AR, 0:1, lanes]
        ai = tab_ref[T_AI, 0:1, lanes]
        hr = h0r_ref[:, lanes]
        hi = h0i_ref[:, lanes]
        for t in range(steps):
            rs = slice(t * tb, (t + 1) * tb)
            hr, hi = ar * hr - ai * hi + xr[rs, :], ar * hi + ai * hr + xi[rs, :]
            xr[rs, :] = hr
            xi[rs, :] = hi
        hre_ref[:, lanes] = hr
        him_ref[:, lanes] = hi
        yss[:, OUT_TILE * o:OUT_TILE * (o + 1)] = _c_projection(xr, xi, cc_ref, o)

    ysn = _ssm_post(yss[...], s, d_ref[...], wglu_ref, gs_ref[...])
    mix_ref[:, :, D_CONV:D_CONV + D_SSM] = ysn.astype(BF16).reshape(steps, tb, D_SSM)


def _mixer_sample(proj, sct, h0r, h0i, cwb, cb, lg, lb, gc, bre, bim, cc, tab, dvec, wglu, gs, tb):
    steps, batch, _ = proj.shape
    const2 = lambda i: (0, 0)
    const3 = lambda i: (0, 0, 0)
    vec = pl.BlockSpec((1, D_CONV), const2)
    col = lambda c: (lambda i: (0, i, c))
    st = pl.BlockSpec((tb, N_STATE), lambda i: (i, 0))
    return pl.pallas_call(
        functools.partial(_mixer_sample_kernel, steps=steps, tb=tb),
        out_shape=(
            jax.ShapeDtypeStruct((steps, batch, D_MODEL), BF16),
            jax.ShapeDtypeStruct((HIST, batch, D_CONV), F32),
            jax.ShapeDtypeStruct((batch, N_STATE), F32),
            jax.ShapeDtypeStruct((batch, N_STATE), F32),
        ),
        grid=(batch // tb,),
        in_specs=[
            pl.BlockSpec((steps, tb, D_CONV), col(0)),
            pl.BlockSpec((steps, tb, D_CONV), col(1)),
            pl.BlockSpec((steps, tb, D_SSM), col(2)),
            pl.BlockSpec((HIST, tb, D_CONV), lambda i: (0, i, 0)),
            st, st,
            pl.BlockSpec(cwb.shape, const3),
            vec, vec, vec, vec,
            pl.BlockSpec(bre.shape, const3),
            pl.BlockSpec(bim.shape, const3),
            pl.BlockSpec(cc.shape, const3),
            pl.BlockSpec(tab.shape, const3),
            vec,
            pl.BlockSpec((D_SSM, D_SSM), const2),
            vec,
        ],
        out_specs=(
            pl.BlockSpec((steps, tb, D_MODEL), lambda i: (0, i, 0)),
            pl.BlockSpec((HIST, tb, D_CONV), lambda i: (0, i, 0)),
            st, st,
        ),
        scratch_shapes=[
            pltpu.VMEM((steps * tb, STATES_PER_OUT_TILE), F32),
            pltpu.VMEM((steps * tb, STATES_PER_OUT_TILE), F32),
            pltpu.VMEM((steps * tb, D_SSM), F32),
        ],
        compiler_params=pltpu.CompilerParams(
            dimension_semantics=("arbitrary",), vmem_limit_bytes=VMEM_LIMIT),
        name="mixer_sample",
    )(proj, proj, proj, sct, h0r, h0i, cwb, cb, lg, lb, gc, bre, bim, cc, tab, dvec, wglu, gs)


def _ssm_tables(a_re, a_im, log_dt, b_re, b_im, c_re, c_im, seg):
    ar = a_re.astype(F32)
    ai = a_im.astype(F32)
    dt = jnp.exp(log_dt.astype(F32))[:, None]

    def power(n):
        mag = jnp.exp(n * dt * ar)
        return (mag * jnp.cos(n * dt * ai)).reshape(N_STATE), (mag * jnp.sin(n * dt * ai)).reshape(N_STATE)

    mag = jnp.exp(dt * ar)
    ab_re = mag * jnp.cos(dt * ai)
    ab_im = mag * jnp.sin(dt * ai)
    den = ar * ar + ai * ai
    nr = ab_re - 1.0
    ni = ab_im
    k_re = (nr * ar + ni * ai) / den
    k_im = (ni * ar - nr * ai) / den
    br = b_re.astype(F32)
    bi = b_im.astype(F32)
    bb_re = k_re[..., None] * br - k_im[..., None] * bi
    bb_im = k_re[..., None] * bi + k_im[..., None] * br

    row = jnp.arange(SUBLANES)[:, None]
    tabs = []
    for shift in (1, 2, 4):
        pr, pi = power(float(shift * seg))
        keep = row >= shift
        tabs += [jnp.where(keep, pr[None, :], 0.0), jnp.where(keep, pi[None, :], 0.0)]
    for first in (0, 1):
        pws = [power(float((i + first) * seg)) for i in range(SUBLANES)]
        tabs += [jnp.stack([p[0] for p in pws]), jnp.stack([p[1] for p in pws])]
    tabs += [jnp.broadcast_to(ab_re.reshape(1, N_STATE), (SUBLANES, N_STATE)),
             jnp.broadcast_to(ab_im.reshape(1, N_STATE), (SUBLANES, N_STATE))]
    tab = jnp.stack(tabs).astype(F32)

    groups_per_tile = STATE_TILE // SSM_STATE
    eye = jnp.eye(groups_per_tile, dtype=F32)

    def b_tiles(bb):
        bt = jnp.transpose(bb, (0, 2, 1)).reshape(N_STATE_TILES, groups_per_tile, SSM_GROUP_CH, SSM_STATE)
        blk = bt[:, :, :, None, :] * eye[None, :, None, :, None]
        blk = blk.reshape(N_STATE_TILES, groups_per_tile * SSM_GROUP_CH, STATE_TILE)
        halves = B_KSLICE // (groups_per_tile * SSM_GROUP_CH)
        sel = (jnp.arange(N_STATE_TILES)[:, None] % halves) == jnp.arange(halves)[None, :]
        out = blk[:, None, :, :] * sel[:, :, None, None].astype(F32)
        return out.reshape(N_STATE_TILES, B_KSLICE, STATE_TILE).astype(BF16)

    groups_per_out = OUT_TILE // SSM_GROUP_CH
    eye_o = jnp.eye(groups_per_out, dtype=F32)

    def c_rows(cm):
        ct = jnp.transpose(cm.astype(F32), (0, 2, 1)).reshape(N_OUT_TILES, groups_per_out, SSM_STATE, SSM_GROUP_CH)
        blk = ct[:, :, :, None, :] * eye_o[None, :, None, :, None]
        return blk.reshape(N_OUT_TILES, STATES_PER_OUT_TILE, OUT_TILE)

    cc = jnp.concatenate([c_rows(c_re), -c_rows(c_im)], axis=1).astype(BF16)
    return tab, b_tiles(bb_re), b_tiles(bb_im), cc


def _segment_permutation(tc):
    seg = tc // SUBLANES
    r = jnp.arange(tc)
    src = (r % SUBLANES) * seg + r // SUBLANES
    return (src[:, None] == jnp.arange(tc)[None, :]).astype(BF16)


def kernel(x_prompt, x_sample, state_conv, state_ssm_re, state_ssm_im, norm_mix, w_in, conv_w, conv_b, conv_ln_g, conv_ln_b, ssm_A_re, ssm_A_im, ssm_log_dt, ssm_B_re, ssm_B_im, ssm_C_re, ssm_C_im, ssm_D, w_glu, gnorm_conv, gnorm_ssm, w_out, norm_ffn, w_ffn_gate, w_ffn_up, w_ffn_down, norm_final):
    depth = norm_mix.shape[0]
    batch, seq, _ = x_prompt.shape
    dbatch, dsteps, _ = x_sample.shape
    tc = 256

    hp = x_prompt.reshape(batch * seq, D_MODEL)
    hs = jnp.transpose(x_sample, (1, 0, 2)).reshape(dsteps * dbatch, D_MODEL)
    perm = _segment_permutation(tc)

    pc, pr, pi_, sc, sr, si = [], [], [], [], [], []
    for d in range(depth):
        tab, bre, bim, cc = _ssm_tables(ssm_A_re[d], ssm_A_im[d], ssm_log_dt[d],
                                        ssm_B_re[d], ssm_B_im[d], ssm_C_re[d], ssm_C_im[d],
                                        seg=tc // SUBLANES)
        vec = lambda a: a[d].reshape(1, -1).astype(F32)
        w_in_b = w_in[d].astype(BF16)
        wglu_b = w_glu[d].astype(BF16)
        wo_b = w_out[d].astype(BF16)
        wg_b = w_ffn_gate[d].astype(BF16)
        wu_b = w_ffn_up[d].astype(BF16)
        wd_b = w_ffn_down[d].astype(BF16)
        cwb = jnp.broadcast_to(conv_w[d].astype(F32)[:, None, :], (CONV_WIDTH, SUBLANES, D_CONV))
        mixer_params = (cwb, vec(conv_b), vec(conv_ln_g), vec(conv_ln_b), vec(gnorm_conv),
                        bre, bim, cc, tab, vec(ssm_D), wglu_b, vec(gnorm_ssm))
        last = d == depth - 1
        nfin = norm_final.reshape(1, -1).astype(F32)

        proj = _in_proj(hp, vec(norm_mix), w_in_b, tm=512)
        mix, cst, hre, him = _mixer_prompt(proj, batch, seq, *mixer_params, perm, tc=tc)
        hp = _ffn(mix, hp, wo_b, vec(norm_ffn), wg_b, wu_b, wd_b, nfin, tm=512, final_norm=last)
        pc.append(cst)
        pr.append(hre.reshape(batch, N_GROUPS, SSM_STATE))
        pi_.append(him.reshape(batch, N_GROUPS, SSM_STATE))

        proj = _in_proj(hs, vec(norm_mix), w_in_b, tm=512).reshape(dsteps, dbatch, D_IN)
        sct = jnp.transpose(state_conv[d], (1, 0, 2))
        h0r = state_ssm_re[d].reshape(dbatch, N_STATE)
        h0i = state_ssm_im[d].reshape(dbatch, N_STATE)
        mix, nct, hre, him = _mixer_sample(proj, sct, h0r, h0i, *mixer_params, tb=32)
        hs = _ffn(mix.reshape(dsteps * dbatch, D_MODEL), hs, wo_b, vec(norm_ffn), wg_b, wu_b, wd_b, nfin,
                  tm=512, final_norm=last)
        sc.append(jnp.transpose(nct, (1, 0, 2)))
        sr.append(hre.reshape(dbatch, N_GROUPS, SSM_STATE))
        si.append(him.reshape(dbatch, N_GROUPS, SSM_STATE))

    y_prompt = hp.reshape(batch, seq, D_MODEL)
    y_sample = jnp.transpose(hs.reshape(dsteps, dbatch, D_MODEL), (1, 0, 2))
    return (y_prompt, y_sample, jnp.stack(pc), jnp.stack(pr), jnp.stack(pi_),
            jnp.stack(sc), jnp.stack(sr), jnp.stack(si))
```

```python
import functools
import math

import jax
import jax.numpy as jnp
from jax import lax
from jax.experimental import pallas as pl
from jax.experimental.pallas import tpu as pltpu

F32 = jnp.float32
BF16 = jnp.bfloat16

D_MODEL = 2048
D_CONV = 1024
D_SSM = 1024
D_IN = 2 * D_CONV + D_SSM
CONV_WIDTH = 31
HIST = CONV_WIDTH - 1
SSM_GROUP_CH = 16
N_GROUPS = D_SSM // SSM_GROUP_CH
SSM_STATE = 64
N_STATE = N_GROUPS * SSM_STATE
D_FF = 5632
EPS = 1e-6

SUBLANES = 8
STATE_TILE = 256
N_STATE_TILES = N_STATE // STATE_TILE
B_KSLICE = 128
OUT_TILE = 256
N_OUT_TILES = D_SSM // OUT_TILE
STATES_PER_OUT_TILE = N_STATE // N_OUT_TILES
VMEM_LIMIT = 56 * 1024 * 1024

(T_B1R, T_B1I, T_B2R, T_B2I, T_B4R, T_B4I,
 T_PSR, T_PSI,
 T_PNR, T_PNI,
 T_AR, T_AI) = range(12)


def _sigmoid(x):
    return 1.0 / (1.0 + jnp.exp(-x))


def _rms_scale(x):
    return lax.rsqrt(jnp.mean(x * x, axis=-1, keepdims=True) + EPS)


def _gelu_tanh(x):
    c = math.sqrt(2.0 / math.pi)
    return x * (0.5 * (1.0 + jnp.tanh(c * (x + 0.044715 * (x * x * x)))))


def _cmul(ar, ai, xr, xi):
    return ar * xr - ai * xi, ar * xi + ai * xr


def _resident(shape):
    zeros = (0,) * len(shape)
    return pl.BlockSpec(shape, lambda *_: zeros, pipeline_mode=pl.Buffered(1))


def _ffn_kernel(mix_ref, x_ref, wo_ref, nf_ref, wg_ref, wu_ref, wd_ref, nfin_ref, o_ref, hn_ref,
                *, final_norm):
    f = pl.program_id(1)

    @pl.when(f == 0)
    def _():
        h = x_ref[...] + jnp.dot(mix_ref[...], wo_ref[...], preferred_element_type=F32)
        o_ref[...] = h
        hn_ref[...] = (h * _rms_scale(h) * nf_ref[...]).astype(BF16)

    hn = hn_ref[...]
    gate = jnp.dot(hn, wg_ref[...], preferred_element_type=F32)
    up = jnp.dot(hn, wu_ref[...], preferred_element_type=F32)
    act = (gate * _sigmoid(gate)) * up
    o_ref[...] += jnp.dot(act.astype(BF16), wd_ref[...], preferred_element_type=F32)

    if final_norm:
        @pl.when(f == pl.num_programs(1) - 1)
        def _():
            h = o_ref[...]
            o_ref[...] = h * _rms_scale(h) * nfin_ref[...]


def _ffn(mix, x, wo, nf, wg, wu, wd, nfin, tm, final_norm):
    rows = x.shape[0]
    tf = 512
    return pl.pallas_call(
        functools.partial(_ffn_kernel, final_norm=final_norm),
        out_shape=jax.ShapeDtypeStruct((rows, D_MODEL), F32),
        grid=(rows // tm, D_FF // tf),
        in_specs=[
            pl.BlockSpec((tm, D_MODEL), lambda i, f: (i, 0)),
            pl.BlockSpec((tm, D_MODEL), lambda i, f: (i, 0)),
            pl.BlockSpec((D_MODEL, D_MODEL), lambda i, f: (0, 0)),
            pl.BlockSpec((1, D_MODEL), lambda i, f: (0, 0)),
            pl.BlockSpec((D_MODEL, tf), lambda i, f: (0, f)),
            pl.BlockSpec((D_MODEL, tf), lambda i, f: (0, f)),
            pl.BlockSpec((tf, D_MODEL), lambda i, f: (f, 0)),
            pl.BlockSpec((1, D_MODEL), lambda i, f: (0, 0)),
        ],
        out_specs=pl.BlockSpec((tm, D_MODEL), lambda i, f: (i, 0)),
        scratch_shapes=[pltpu.VMEM((tm, D_MODEL), BF16)],
        compiler_params=pltpu.CompilerParams(
            dimension_semantics=("arbitrary", "arbitrary"), vmem_limit_bytes=VMEM_LIMIT),
        name="ffn",
    )(mix, x, wo, nf, wg, wu, wd, nfin)


def _conv_post(y, lg, lb, gc):
    mu = jnp.mean(y, axis=-1, keepdims=True)
    yc = y - mu
    var = jnp.mean(yc * yc, axis=-1, keepdims=True)
    yl = yc * lax.rsqrt(var + EPS) * lg + lb
    ys = yl * _sigmoid(yl)
    return ys * _rms_scale(ys) * gc


def _b_projection(sb, bre_ref, bim_ref, xr, xi, o):
    per = STATES_PER_OUT_TILE // STATE_TILE
    for i in range(per):
        q = per * o + i
        first_channel = q * STATE_TILE // SSM_STATE * SSM_GROUP_CH
        ks = first_channel // B_KSLICE * B_KSLICE
        lhs = sb[:, ks:ks + B_KSLICE]
        xr[:, STATE_TILE * i:STATE_TILE * (i + 1)] = jnp.dot(
            lhs, bre_ref[q], preferred_element_type=F32)
        xi[:, STATE_TILE * i:STATE_TILE * (i + 1)] = jnp.dot(
            lhs, bim_ref[q], preferred_element_type=F32)


def _c_projection(xr, xi, cc_ref, o):
    hcat = jnp.concatenate([xr[...].astype(BF16), xi[...].astype(BF16)], axis=1)
    return jnp.dot(hcat, cc_ref[o], preferred_element_type=F32)


def _ssm_post(yss, s, d, wglu_ref, gs):
    y = _gelu_tanh(yss + d * s)
    z = jnp.dot(y.astype(BF16), wglu_ref[...], preferred_element_type=F32)
    y = y * _sigmoid(z)
    return y * _rms_scale(y) * gs


def _front_prompt_kernel(x_ref, nm_ref, win_ref, perm_ref, permt_ref,
                         cwb_ref, cb_ref, lg_ref, lb_ref, gc_ref,
                         bre_ref, bim_ref, cc_ref, tab_ref, d_ref, wglu_ref, gs_ref,
                         mix_ref, cst_ref, hre_ref, him_ref,
                         proj, xps, uext, uprev, ycv, scur, xr0, xi0, xr1, xi1, hsr, hsi, yss,
                         *, tc, nc):
    g = pl.program_id(0)
    m = jnp.maximum(g - 1, 0)
    c = m % nc
    seg = tc // SUBLANES

    @pl.when(g == 0)
    def _():
        proj[...] = jnp.zeros_like(proj)

    @pl.when(c == 0)
    def _():
        uprev[...] = jnp.zeros_like(uprev)
        hsr[...] = jnp.zeros_like(hsr)
        hsi[...] = jnp.zeros_like(hsi)

    def tile(i):
        return slice(i * SUBLANES, (i + 1) * SUBLANES)

    ucur = uext.at[pl.ds(HIST * SUBLANES, tc)]
    per = D_CONV // STATE_TILE
    for k in range(per):
        cols = slice(k * STATE_TILE, (k + 1) * STATE_TILE)
        ucur[:, cols] = proj[k] * _sigmoid(proj[per + k])
        scur[:, cols] = proj[2 * per + k]
    sb = scur[...].astype(BF16)

    row_id = lax.broadcasted_iota(jnp.int32, (SUBLANES, D_CONV), 0)
    for i in range(seg - HIST, seg):
        uext[tile(i - (seg - HIST)), :] = pltpu.roll(
            jnp.where(row_id == SUBLANES - 1, uprev[tile(i), :], ucur[tile(i), :]), 1, 0)

    x = x_ref[...]
    xn = (x * _rms_scale(x) * nm_ref[...]).astype(BF16)
    xps[...] = jnp.dot(perm_ref[...], xn, preferred_element_type=F32).astype(BF16)

    lw = 256
    ob = 4
    trips = 4
    slabs_per_trip = D_IN // STATE_TILE // trips
    groups_per_trip = seg // ob // trips

    def conv_body(it, carry):
        for r in range(slabs_per_trip):
            n = it * slabs_per_trip + r
            proj[n] = jnp.dot(xps[...], win_ref[n], preferred_element_type=F32)
        for grp in range(groups_per_trip):
            j0 = (it * groups_per_trip + grp) * ob
            for lh in range(D_CONV // lw):
                ls = slice(lh * lw, (lh + 1) * lw)
                acc = [None] * ob
                for rel in range(ob - 1, -CONV_WIDTH, -1):
                    uses = [(jj, jj - rel) for jj in range(ob) if 0 <= jj - rel <= HIST]
                    row0 = pl.multiple_of((j0 + rel + HIST) * SUBLANES, SUBLANES)
                    src = uext[pl.ds(row0, SUBLANES), ls]
                    for jj, d in uses:
                        term = src * cwb_ref[HIST - d, :, ls]
                        acc[jj] = term if acc[jj] is None else acc[jj] + term
                for jj in range(ob):
                    row0 = pl.multiple_of((j0 + jj) * SUBLANES, SUBLANES)
                    ycv[pl.ds(row0, SUBLANES), ls] = acc[jj]
        return carry

    lax.fori_loop(0, trips, conv_body, 0)

    ends_sequence = jnp.logical_and(g > 0, c == nc - 1)

    @pl.when(ends_sequence)
    def _():
        for r in range(HIST):
            row = (r + seg - HIST) * SUBLANES + SUBLANES - 1
            cst_ref[0, r:r + 1, :] = ucur[row:row + 1, :]

    uprev[...] = ucur[...]
    ycn = _conv_post(ycv[...] + cb_ref[...], lg_ref[...], lb_ref[...], gc_ref[...])
    mix_ref[:, 0:D_CONV] = jnp.dot(
        permt_ref[...], ycn.astype(BF16), preferred_element_type=F32).astype(BF16)

    keep1 = lax.broadcasted_iota(jnp.int32, (SUBLANES, STATES_PER_OUT_TILE), 0) >= 1
    bufs = ((xr0, xi0), (xr1, xi1))
    _b_projection(sb, bre_ref, bim_ref, *bufs[0], 0)

    for o in range(N_OUT_TILES):
        xr, xi = bufs[o % 2]
        if o + 1 < N_OUT_TILES:
            _b_projection(sb, bre_ref, bim_ref, *bufs[(o + 1) % 2], o + 1)
        lanes = slice(STATES_PER_OUT_TILE * o, STATES_PER_OUT_TILE * (o + 1))
        ar = tab_ref[T_AR, :, lanes]
        ai = tab_ref[T_AI, :, lanes]

        fr = xr[tile(0), :]
        fi = xi[tile(0), :]
        for j in range(1, seg):
            pr, pi = _cmul(ar, ai, fr, fi)
            fr = pr + xr[tile(j), :]
            fi = pi + xi[tile(j), :]

        for shift, ia, ib in ((1, T_B1R, T_B1I), (2, T_B2R, T_B2I), (4, T_B4R, T_B4I)):
            pr, pi = _cmul(tab_ref[ia, :, lanes], tab_ref[ib, :, lanes],
                           pltpu.roll(fr, shift, 0), pltpu.roll(fi, shift, 0))
            fr, fi = fr + pr, fi + pi
        cr = hsr[:, lanes]
        ci = hsi[:, lanes]
        pr, pi = _cmul(tab_ref[T_PSR, :, lanes], tab_ref[T_PSI, :, lanes], cr, ci)
        hr = pr + jnp.where(keep1, pltpu.roll(fr, 1, 0), 0.0)
        hi = pi + jnp.where(keep1, pltpu.roll(fi, 1, 0), 0.0)
        pr, pi = _cmul(tab_ref[T_PNR, :, lanes], tab_ref[T_PNI, :, lanes], cr, ci)
        hsr[:, lanes] = jnp.broadcast_to((pr + fr)[SUBLANES - 1:SUBLANES, :], cr.shape)
        hsi[:, lanes] = jnp.broadcast_to((pi + fi)[SUBLANES - 1:SUBLANES, :], ci.shape)

        for j in range(seg):
            pr, pi = _cmul(ar, ai, hr, hi)
            hr = pr + xr[tile(j), :]
            hi = pi + xi[tile(j), :]
            xr[tile(j), :] = hr
            xi[tile(j), :] = hi
        yss[:, OUT_TILE * o:OUT_TILE * (o + 1)] = _c_projection(xr, xi, cc_ref, o)

    ysn = _ssm_post(yss[...], scur[...], d_ref[...], wglu_ref, gs_ref[...])
    mix_ref[:, D_CONV:D_CONV + D_SSM] = jnp.dot(
        permt_ref[...], ysn.astype(BF16), preferred_element_type=F32).astype(BF16)

    @pl.when(ends_sequence)
    def _():
        hre_ref[0] = hsr[0:1, :]
        him_ref[0] = hsi[0:1, :]


def _front_prompt(x, batch, seq, nm, w_in, perm, cwb, cb, lg, lb, gc, bre, bim, cc, tab, dvec, wglu, gs, tc):
    nc = seq // tc
    total = batch * nc
    vec = _resident((1, D_CONV))
    mixed = lambda g: jnp.maximum(g - 1, 0)
    return pl.pallas_call(
        functools.partial(_front_prompt_kernel, tc=tc, nc=nc),
        out_shape=(
            jax.ShapeDtypeStruct((batch * seq, D_MODEL), BF16),
            jax.ShapeDtypeStruct((batch, HIST, D_CONV), F32),
            jax.ShapeDtypeStruct((batch, 1, N_STATE), F32),
            jax.ShapeDtypeStruct((batch, 1, N_STATE), F32),
        ),
        grid=(total + 1,),
        in_specs=[
            pl.BlockSpec((tc, D_MODEL), lambda g: (jnp.minimum(g, total - 1), 0)),
            _resident((1, D_MODEL)),
            _resident((D_IN // STATE_TILE, D_MODEL, STATE_TILE)),
            _resident((tc, tc)),
            _resident((tc, tc)),
            _resident(cwb.shape),
            vec, vec, vec, vec,
            _resident(bre.shape),
            _resident(bim.shape),
            _resident(cc.shape),
            _resident(tab.shape),
            vec,
            _resident((D_SSM, D_SSM)),
            vec,
        ],
        out_specs=(
            pl.BlockSpec((tc, D_MODEL), lambda g: (mixed(g), 0)),
            pl.BlockSpec((1, HIST, D_CONV), lambda g: (mixed(g) // nc, 0, 0)),
            pl.BlockSpec((1, 1, N_STATE), lambda g: (mixed(g) // nc, 0, 0)),
            pl.BlockSpec((1, 1, N_STATE), lambda g: (mixed(g) // nc, 0, 0)),
        ),
        scratch_shapes=[
            pltpu.VMEM((D_IN // STATE_TILE, tc, STATE_TILE), F32),
            pltpu.VMEM((tc, D_MODEL), BF16),
            pltpu.VMEM((HIST * SUBLANES + tc, D_CONV), F32),
            pltpu.VMEM((tc, D_CONV), F32),
            pltpu.VMEM((tc, D_CONV), F32),
            pltpu.VMEM((tc, D_SSM), F32),
            pltpu.VMEM((tc, STATES_PER_OUT_TILE), F32),
            pltpu.VMEM((tc, STATES_PER_OUT_TILE), F32),
            pltpu.VMEM((tc, STATES_PER_OUT_TILE), F32),
            pltpu.VMEM((tc, STATES_PER_OUT_TILE), F32),
            pltpu.VMEM((SUBLANES, N_STATE), F32),
            pltpu.VMEM((SUBLANES, N_STATE), F32),
            pltpu.VMEM((tc, D_SSM), F32),
        ],
        compiler_params=pltpu.CompilerParams(
            dimension_semantics=("arbitrary",), vmem_limit_bytes=VMEM_LIMIT),
        name="front_prompt",
    )(x, nm, w_in, perm, perm.T, cwb, cb, lg, lb, gc, bre, bim, cc, tab, dvec, wglu, gs)


def _front_sample_kernel(x_ref, sc_ref, h0r_ref, h0i_ref, nm_ref, win_ref,
                         cwb_ref, cb_ref, lg_ref, lb_ref, gc_ref,
                         bre_ref, bim_ref, cc_ref, tab_ref, d_ref, wglu_ref, gs_ref,
                         mix_ref, nc_ref, hre_ref, him_ref,
                         xns, ucur, scur, xr, xi, yss, *, steps, tb):
    def at(t, width):
        return slice(t * width, (t + 1) * width)

    for t in range(steps):
        x = x_ref[:, at(t, D_MODEL)]
        xns[at(t, tb), :] = (x * _rms_scale(x) * nm_ref[...]).astype(BF16)
    per = D_CONV // STATE_TILE
    for k in range(per):
        cols = at(k, STATE_TILE)
        v = jnp.dot(xns[...], win_ref[k], preferred_element_type=F32)
        gt = jnp.dot(xns[...], win_ref[per + k], preferred_element_type=F32)
        ucur[:, cols] = v * _sigmoid(gt)
        scur[:, cols] = jnp.dot(xns[...], win_ref[2 * per + k], preferred_element_type=F32)

    def window(j):
        return sc_ref[:, at(j, D_CONV)] if j < HIST else ucur[at(j - HIST, tb), :]

    for t in range(steps):
        acc = jnp.zeros((tb // SUBLANES, SUBLANES, D_CONV), F32)
        for k in range(CONV_WIDTH):
            acc = acc + window(t + k).reshape(tb // SUBLANES, SUBLANES, D_CONV) * cwb_ref[k][None]
        ycn = _conv_post(acc.reshape(tb, D_CONV) + cb_ref[...], lg_ref[...], lb_ref[...], gc_ref[...])
        mix_ref[:, t * D_MODEL:t * D_MODEL + D_CONV] = ycn.astype(BF16)
    for j in range(HIST):
        nc_ref[:, at(j, D_CONV)] = window(steps + j)

    s = scur[...]
    sb = s.astype(BF16)
    for o in range(N_OUT_TILES):
        _b_projection(sb, bre_ref, bim_ref, xr, xi, o)
        lanes = slice(STATES_PER_OUT_TILE * o, STATES_PER_OUT_TILE * (o + 1))
        ar = tab_ref[T_AR, 0:1, lanes]
        ai = tab_ref[T_AI, 0:1, lanes]
        hr = h0r_ref[:, lanes]
        hi = h0i_ref[:, lanes]
        for t in range(steps):
            rs = slice(t * tb, (t + 1) * tb)
            hr, hi = ar * hr - ai * hi + xr[rs, :], ar * hi + ai * hr + xi[rs, :]
            xr[rs, :] = hr
            xi[rs, :] = hi
        hre_ref[:, lanes] = hr
        him_ref[:, lanes] = hi
        yss[:, OUT_TILE * o:OUT_TILE * (o + 1)] = _c_projection(xr, xi, cc_ref, o)

    ysn = _ssm_post(yss[...], s, d_ref[...], wglu_ref, gs_ref[...]).astype(BF16)
    for t in range(steps):
        mix_ref[:, t * D_MODEL + D_CONV:(t + 1) * D_MODEL] = ysn[at(t, tb), :]


def _front_sample(x, sc, h0r, h0i, nm, w_in, cwb, cb, lg, lb, gc, bre, bim, cc, tab, dvec, wglu, gs, steps, tb):
    batch = x.shape[0]
    vec = _resident((1, D_CONV))
    rows = lambda width: pl.BlockSpec((tb, width), lambda i: (i, 0))
    return pl.pallas_call(
        functools.partial(_front_sample_kernel, steps=steps, tb=tb),
        out_shape=(
            jax.ShapeDtypeStruct((batch, steps * D_MODEL), BF16),
            jax.ShapeDtypeStruct((batch, HIST * D_CONV), F32),
            jax.ShapeDtypeStruct((batch, N_STATE), F32),
            jax.ShapeDtypeStruct((batch, N_STATE), F32),
        ),
        grid=(batch // tb,),
        in_specs=[
            rows(steps * D_MODEL),
            rows(HIST * D_CONV),
            rows(N_STATE), rows(N_STATE),
            _resident((1, D_MODEL)),
            _resident(w_in.shape),
            _resident(cwb.shape),
            vec, vec, vec, vec,
            _resident(bre.shape),
            _resident(bim.shape),
            _resident(cc.shape),
            _resident(tab.shape),
            vec,
            _resident((D_SSM, D_SSM)),
            vec,
        ],
        out_specs=(
            rows(steps * D_MODEL),
            rows(HIST * D_CONV),
            rows(N_STATE), rows(N_STATE),
        ),
        scratch_shapes=[
            pltpu.VMEM((steps * tb, D_MODEL), BF16),
            pltpu.VMEM((steps * tb, D_CONV), F32),
            pltpu.VMEM((steps * tb, D_SSM), F32),
            pltpu.VMEM((steps * tb, STATES_PER_OUT_TILE), F32),
            pltpu.VMEM((steps * tb, STATES_PER_OUT_TILE), F32),
            pltpu.VMEM((steps * tb, D_SSM), F32),
        ],
        compiler_params=pltpu.CompilerParams(
            dimension_semantics=("arbitrary",), vmem_limit_bytes=VMEM_LIMIT),
        name="front_sample",
    )(x, sc, h0r, h0i, nm, w_in, cwb, cb, lg, lb, gc, bre, bim, cc, tab, dvec, wglu, gs)


def _ssm_tables(a_re, a_im, log_dt, b_re, b_im, c_re, c_im, seg):
    ar = a_re.astype(F32)
    ai = a_im.astype(F32)
    dt = jnp.exp(log_dt.astype(F32))[:, None]

    mag = jnp.exp(dt * ar)
    ab_re = mag * jnp.cos(dt * ai)
    ab_im = mag * jnp.sin(dt * ai)
    den = ar * ar + ai * ai
    nr = ab_re - 1.0
    ni = ab_im
    k_re = (nr * ar + ni * ai) / den
    k_im = (ni * ar - nr * ai) / den
    br = b_re.astype(F32)
    bi = b_im.astype(F32)
    bb_re = k_re[..., None] * br - k_im[..., None] * bi
    bb_im = k_re[..., None] * bi + k_im[..., None] * br

    n = (jnp.arange(SUBLANES + 1, dtype=F32) * seg)[:, None, None]
    pmag = jnp.exp(n * dt * ar)
    pw_re = (pmag * jnp.cos(n * dt * ai)).reshape(SUBLANES + 1, N_STATE)
    pw_im = (pmag * jnp.sin(n * dt * ai)).reshape(SUBLANES + 1, N_STATE)
    row = jnp.arange(SUBLANES)[:, None]
    tabs = []
    for shift in (1, 2, 4):
        keep = row >= shift
        tabs += [jnp.where(keep, pw_re[shift][None, :], 0.0), jnp.where(keep, pw_im[shift][None, :], 0.0)]
    tabs += [pw_re[:SUBLANES], pw_im[:SUBLANES], pw_re[1:], pw_im[1:]]
    tabs += [jnp.broadcast_to(ab_re.reshape(1, N_STATE), (SUBLANES, N_STATE)),
             jnp.broadcast_to(ab_im.reshape(1, N_STATE), (SUBLANES, N_STATE))]
    tab = jnp.stack(tabs).astype(F32)

    groups_per_tile = STATE_TILE // SSM_STATE
    eye = jnp.eye(groups_per_tile, dtype=F32)

    def b_tiles(bb):
        bt = jnp.transpose(bb, (0, 2, 1)).reshape(N_STATE_TILES, groups_per_tile, SSM_GROUP_CH, SSM_STATE)
        blk = bt[:, :, :, None, :] * eye[None, :, None, :, None]
        blk = blk.reshape(N_STATE_TILES, groups_per_tile * SSM_GROUP_CH, STATE_TILE)
        halves = B_KSLICE // (groups_per_tile * SSM_GROUP_CH)
        sel = (jnp.arange(N_STATE_TILES)[:, None] % halves) == jnp.arange(halves)[None, :]
        out = blk[:, None, :, :] * sel[:, :, None, None].astype(F32)
        return out.reshape(N_STATE_TILES, B_KSLICE, STATE_TILE).astype(BF16)

    groups_per_out = OUT_TILE // SSM_GROUP_CH
    eye_o = jnp.eye(groups_per_out, dtype=F32)

    def c_rows(cm):
        ct = jnp.transpose(cm.astype(F32), (0, 2, 1)).reshape(N_OUT_TILES, groups_per_out, SSM_STATE, SSM_GROUP_CH)
        blk = ct[:, :, :, None, :] * eye_o[None, :, None, :, None]
        return blk.reshape(N_OUT_TILES, STATES_PER_OUT_TILE, OUT_TILE)

    cc = jnp.concatenate([c_rows(c_re), -c_rows(c_im)], axis=1).astype(BF16)
    return tab, b_tiles(bb_re), b_tiles(bb_im), cc


def _segment_permutation(tc):
    seg = tc // SUBLANES
    r = jnp.arange(tc)
    src = (r % SUBLANES) * seg + r // SUBLANES
    return (src[:, None] == jnp.arange(tc)[None, :]).astype(BF16)


def kernel(x_prompt, x_sample, state_conv, state_ssm_re, state_ssm_im, norm_mix, w_in, conv_w, conv_b, conv_ln_g, conv_ln_b, ssm_A_re, ssm_A_im, ssm_log_dt, ssm_B_re, ssm_B_im, ssm_C_re, ssm_C_im, ssm_D, w_glu, gnorm_conv, gnorm_ssm, w_out, norm_ffn, w_ffn_gate, w_ffn_up, w_ffn_down, norm_final):
    depth = norm_mix.shape[0]
    batch, seq, _ = x_prompt.shape
    dbatch, dsteps, _ = x_sample.shape
    tc = 256
    assert HIST <= tc // SUBLANES and seq % tc == 0

    hp = x_prompt.reshape(batch * seq, D_MODEL)
    hs = x_sample.reshape(dbatch * dsteps, D_MODEL)
    perm = _segment_permutation(tc)

    pc, pr, pi_, sc, sr, si = [], [], [], [], [], []
    for d in range(depth):
        tab, bre, bim, cc = _ssm_tables(ssm_A_re[d], ssm_A_im[d], ssm_log_dt[d],
                                        ssm_B_re[d], ssm_B_im[d], ssm_C_re[d], ssm_C_im[d],
                                        seg=tc // SUBLANES)
        vec = lambda a: a[d].reshape(1, -1).astype(F32)
        w_in_b = w_in[d].astype(BF16)
        wglu_b = w_glu[d].astype(BF16)
        wo_b = w_out[d].astype(BF16)
        wg_b = w_ffn_gate[d].astype(BF16)
        wu_b = w_ffn_up[d].astype(BF16)
        wd_b = w_ffn_down[d].astype(BF16)
        cwb = jnp.broadcast_to(conv_w[d].astype(F32)[:, None, :], (CONV_WIDTH, SUBLANES, D_CONV))
        mixer_params = (cwb, vec(conv_b), vec(conv_ln_g), vec(conv_ln_b), vec(gnorm_conv),
                        bre, bim, cc, tab, vec(ssm_D), wglu_b, vec(gnorm_ssm))
        last = d == depth - 1
        nfin = norm_final.reshape(1, -1).astype(F32)

        w_in_slabs = jnp.transpose(w_in_b.reshape(D_MODEL, D_IN // STATE_TILE, STATE_TILE), (1, 0, 2))
        mix, cst, hre, him = _front_prompt(hp, batch, seq, vec(norm_mix), w_in_slabs, perm, *mixer_params, tc=tc)
        hp = _ffn(mix, hp, wo_b, vec(norm_ffn), wg_b, wu_b, wd_b, nfin, tm=512, final_norm=last)
        pc.append(cst)
        pr.append(hre.reshape(batch, N_GROUPS, SSM_STATE))
        pi_.append(him.reshape(batch, N_GROUPS, SSM_STATE))

        mix, nconv, hre, him = _front_sample(
            hs.reshape(dbatch, dsteps * D_MODEL), state_conv[d].reshape(dbatch, HIST * D_CONV),
            state_ssm_re[d].reshape(dbatch, N_STATE), state_ssm_im[d].reshape(dbatch, N_STATE),
            vec(norm_mix), w_in_slabs, *mixer_params, steps=dsteps, tb=32)
        hs = _ffn(mix.reshape(dbatch * dsteps, D_MODEL), hs, wo_b, vec(norm_ffn), wg_b, wu_b, wd_b, nfin,
                  tm=512, final_norm=last)
        sc.append(nconv.reshape(dbatch, HIST, D_CONV))
        sr.append(hre.reshape(dbatch, N_GROUPS, SSM_STATE))
        si.append(him.reshape(dbatch, N_GROUPS, SSM_STATE))

    stack = lambda parts: parts[0][None] if len(parts) == 1 else jnp.stack(parts)
    y_prompt = hp.reshape(batch, seq, D_MODEL)
    y_sample = hs.reshape(dbatch, dsteps, D_MODEL)
    return (y_prompt, y_sample, stack(pc), stack(pr), stack(pi_), stack(sc), stack(sr), stack(si))
```

```python
import functools
import math

import jax
import jax.numpy as jnp
from jax import lax
from jax.experimental import pallas as pl
from jax.experimental.pallas import tpu as pltpu

F32 = jnp.float32
BF16 = jnp.bfloat16

D_MODEL = 2048
D_CONV = 1024
D_SSM = 1024
D_IN = 2 * D_CONV + D_SSM
CONV_WIDTH = 31
HIST = CONV_WIDTH - 1
SSM_GROUP_CH = 16
N_GROUPS = D_SSM // SSM_GROUP_CH
SSM_STATE = 64
N_STATE = N_GROUPS * SSM_STATE
D_FF = 5632
EPS = 1e-6

SUBLANES = 8
STATE_TILE = 256
N_STATE_TILES = N_STATE // STATE_TILE
B_KSLICE = 128
OUT_TILE = 256
N_OUT_TILES = D_SSM // OUT_TILE
STATES_PER_OUT_TILE = N_STATE // N_OUT_TILES
VMEM_LIMIT = 56 * 1024 * 1024

(T_B1R, T_B1I, T_B2R, T_B2I, T_B4R, T_B4I,
 T_PSR, T_PSI,
 T_PNR, T_PNI,
 T_AR, T_AI) = range(12)


def _sigmoid(x):
    return 1.0 / (1.0 + jnp.exp(-x))


def _rms_scale(x):
    return lax.rsqrt(jnp.mean(x * x, axis=-1, keepdims=True) + EPS)


def _gelu_tanh(x):
    c = math.sqrt(2.0 / math.pi)
    return x * (0.5 * (1.0 + jnp.tanh(c * (x + 0.044715 * (x * x * x)))))


def _cmul(ar, ai, xr, xi):
    return ar * xr - ai * xi, ar * xi + ai * xr


def _resident(shape):
    zeros = (0,) * len(shape)
    return pl.BlockSpec(shape, lambda *_: zeros, pipeline_mode=pl.Buffered(1))


def _ffn_kernel(mix_ref, x_ref, wo_ref, nf_ref, wg_ref, wu_ref, wd_ref, nfin_ref, o_ref, hn_ref,
                *, final_norm):
    f = pl.program_id(1)

    @pl.when(f == 0)
    def _():
        h = x_ref[...] + jnp.dot(mix_ref[...], wo_ref[...], preferred_element_type=F32)
        o_ref[...] = h
        hn_ref[...] = (h * _rms_scale(h) * nf_ref[...]).astype(BF16)

    hn = hn_ref[...]
    gate = jnp.dot(hn, wg_ref[...], preferred_element_type=F32)
    up = jnp.dot(hn, wu_ref[...], preferred_element_type=F32)
    act = (gate * _sigmoid(gate)) * up
    o_ref[...] += jnp.dot(act.astype(BF16), wd_ref[...], preferred_element_type=F32)

    if final_norm:
        @pl.when(f == pl.num_programs(1) - 1)
        def _():
            h = o_ref[...]
            o_ref[...] = h * _rms_scale(h) * nfin_ref[...]


def _ffn(mix, x, wo, nf, wg, wu, wd, nfin, tm, final_norm):
    rows = x.shape[0]
    tf = 512
    return pl.pallas_call(
        functools.partial(_ffn_kernel, final_norm=final_norm),
        out_shape=jax.ShapeDtypeStruct((rows, D_MODEL), F32),
        grid=(rows // tm, D_FF // tf),
        in_specs=[
            pl.BlockSpec((tm, D_MODEL), lambda i, f: (i, 0)),
            pl.BlockSpec((tm, D_MODEL), lambda i, f: (i, 0)),
            pl.BlockSpec((D_MODEL, D_MODEL), lambda i, f: (0, 0)),
            pl.BlockSpec((1, D_MODEL), lambda i, f: (0, 0)),
            pl.BlockSpec((D_MODEL, tf), lambda i, f: (0, f)),
            pl.BlockSpec((D_MODEL, tf), lambda i, f: (0, f)),
            pl.BlockSpec((tf, D_MODEL), lambda i, f: (f, 0)),
            pl.BlockSpec((1, D_MODEL), lambda i, f: (0, 0)),
        ],
        out_specs=pl.BlockSpec((tm, D_MODEL), lambda i, f: (i, 0)),
        scratch_shapes=[pltpu.VMEM((tm, D_MODEL), BF16)],
        compiler_params=pltpu.CompilerParams(
            dimension_semantics=("arbitrary", "arbitrary"), vmem_limit_bytes=VMEM_LIMIT),
        name="ffn",
    )(mix, x, wo, nf, wg, wu, wd, nfin)


def _conv_post(y, lg, lb, gc):
    mu = jnp.mean(y, axis=-1, keepdims=True)
    yc = y - mu
    var = jnp.mean(yc * yc, axis=-1, keepdims=True)
    yl = yc * lax.rsqrt(var + EPS) * lg + lb
    ys = yl * _sigmoid(yl)
    return ys * _rms_scale(ys) * gc


def _b_projection(sb, bre_ref, bim_ref, xr, xi, o):
    per = STATES_PER_OUT_TILE // STATE_TILE
    for i in range(per):
        q = per * o + i
        first_channel = q * STATE_TILE // SSM_STATE * SSM_GROUP_CH
        ks = first_channel // B_KSLICE * B_KSLICE
        lhs = sb[:, ks:ks + B_KSLICE]
        xr[:, STATE_TILE * i:STATE_TILE * (i + 1)] = jnp.dot(
            lhs, bre_ref[q], preferred_element_type=F32)
        xi[:, STATE_TILE * i:STATE_TILE * (i + 1)] = jnp.dot(
            lhs, bim_ref[q], preferred_element_type=F32)


def _c_projection(xr, xi, cc_ref, o):
    hcat = jnp.concatenate([xr[...].astype(BF16), xi[...].astype(BF16)], axis=1)
    return jnp.dot(hcat, cc_ref[o], preferred_element_type=F32)


def _ssm_post(yss, s, d, wglu_ref, gs):
    y = _gelu_tanh(yss + d * s)
    z = jnp.dot(y.astype(BF16), wglu_ref[...], preferred_element_type=F32)
    y = y * _sigmoid(z)
    return y * _rms_scale(y) * gs


def _front_prompt_kernel(x_ref, nm_ref, win_ref, perm_ref, permt_ref,
                         cwb_ref, cb_ref, lg_ref, lb_ref, gc_ref,
                         bre_ref, bim_ref, cc_ref, tab_ref, d_ref, wglu_ref, gs_ref,
                         mix_ref, cst_ref, hre_ref, him_ref,
                         proj, xps, uext, uprev, ycv, scur, xr0, xi0, xr1, xi1, hsr, hsi, yss,
                         *, tc, nc):
    g = pl.program_id(0)
    m = jnp.maximum(g - 1, 0)
    c = m % nc
    seg = tc // SUBLANES

    @pl.when(g == 0)
    def _():
        proj[...] = jnp.zeros_like(proj)

    @pl.when(c == 0)
    def _():
        uprev[...] = jnp.zeros_like(uprev)
        hsr[...] = jnp.zeros_like(hsr)
        hsi[...] = jnp.zeros_like(hsi)

    def tile(i):
        return slice(i * SUBLANES, (i + 1) * SUBLANES)

    ucur = uext.at[pl.ds(HIST * SUBLANES, tc)]
    per = D_CONV // STATE_TILE
    for k in range(per):
        cols = slice(k * STATE_TILE, (k + 1) * STATE_TILE)
        ucur[:, cols] = proj[k] * _sigmoid(proj[per + k])
        scur[:, cols] = proj[2 * per + k]
    sb = scur[...].astype(BF16)

    row_id = lax.broadcasted_iota(jnp.int32, (SUBLANES, D_CONV), 0)
    for i in range(seg - HIST, seg):
        uext[tile(i - (seg - HIST)), :] = pltpu.roll(
            jnp.where(row_id == SUBLANES - 1, uprev[tile(i), :], ucur[tile(i), :]), 1, 0)

    x = x_ref[...]
    xn = (x * _rms_scale(x) * nm_ref[...]).astype(BF16)
    xps[...] = jnp.dot(perm_ref[...], xn, preferred_element_type=F32).astype(BF16)

    lw = 256
    ob = 4
    trips = 4
    slabs_per_trip = D_IN // STATE_TILE // trips
    groups_per_trip = seg // ob // trips

    def conv_body(it, carry):
        for grp in range(groups_per_trip):
            j0 = (it * groups_per_trip + grp) * ob
            for lh in range(D_CONV // lw):
                ls = slice(lh * lw, (lh + 1) * lw)
                acc = [None] * ob
                for rel in range(ob - 1, -CONV_WIDTH, -1):
                    uses = [(jj, jj - rel) for jj in range(ob) if 0 <= jj - rel <= HIST]
                    row0 = pl.multiple_of((j0 + rel + HIST) * SUBLANES, SUBLANES)
                    src = uext[pl.ds(row0, SUBLANES), ls]
                    for jj, d in uses:
                        term = src * cwb_ref[HIST - d, :, ls]
                        acc[jj] = term if acc[jj] is None else acc[jj] + term
                for jj in range(ob):
                    row0 = pl.multiple_of((j0 + jj) * SUBLANES, SUBLANES)
                    ycv[pl.ds(row0, SUBLANES), ls] = acc[jj]
        for r in range(slabs_per_trip):
            n = it * slabs_per_trip + r
            proj[n] = jnp.dot(xps[...], win_ref[n], preferred_element_type=F32)
        return carry

    lax.fori_loop(0, trips, conv_body, 0)

    ends_sequence = jnp.logical_and(g > 0, c == nc - 1)

    @pl.when(ends_sequence)
    def _():
        for r in range(HIST):
            row = (r + seg - HIST) * SUBLANES + SUBLANES - 1
            cst_ref[0, r:r + 1, :] = ucur[row:row + 1, :]

    uprev[...] = ucur[...]
    ycn = _conv_post(ycv[...] + cb_ref[...], lg_ref[...], lb_ref[...], gc_ref[...])
    mix_ref[:, 0:D_CONV] = jnp.dot(
        permt_ref[...], ycn.astype(BF16), preferred_element_type=F32).astype(BF16)

    keep1 = lax.broadcasted_iota(jnp.int32, (SUBLANES, STATES_PER_OUT_TILE), 0) >= 1
    bufs = ((xr0, xi0), (xr1, xi1))
    _b_projection(sb, bre_ref, bim_ref, *bufs[0], 0)

    for o in range(N_OUT_TILES):
        xr, xi = bufs[o % 2]
        if o + 1 < N_OUT_TILES:
            _b_projection(sb, bre_ref, bim_ref, *bufs[(o + 1) % 2], o + 1)
        lanes = slice(STATES_PER_OUT_TILE * o, STATES_PER_OUT_TILE * (o + 1))
        ar = tab_ref[T_AR, :, lanes]
        ai = tab_ref[T_AI, :, lanes]

        fr = xr[tile(0), :]
        fi = xi[tile(0), :]
        for j in range(1, seg):
            pr, pi = _cmul(ar, ai, fr, fi)
            fr = pr + xr[tile(j), :]
            fi = pi + xi[tile(j), :]

        for shift, ia, ib in ((1, T_B1R, T_B1I), (2, T_B2R, T_B2I), (4, T_B4R, T_B4I)):
            pr, pi = _cmul(tab_ref[ia, :, lanes], tab_ref[ib, :, lanes],
                           pltpu.roll(fr, shift, 0), pltpu.roll(fi, shift, 0))
            fr, fi = fr + pr, fi + pi
        cr = hsr[:, lanes]
        ci = hsi[:, lanes]
        pr, pi = _cmul(tab_ref[T_PSR, :, lanes], tab_ref[T_PSI, :, lanes], cr, ci)
        hr = pr + jnp.where(keep1, pltpu.roll(fr, 1, 0), 0.0)
        hi = pi + jnp.where(keep1, pltpu.roll(fi, 1, 0), 0.0)
        pr, pi = _cmul(tab_ref[T_PNR, :, lanes], tab_ref[T_PNI, :, lanes], cr, ci)
        hsr[:, lanes] = jnp.broadcast_to((pr + fr)[SUBLANES - 1:SUBLANES, :], cr.shape)
        hsi[:, lanes] = jnp.broadcast_to((pi + fi)[SUBLANES - 1:SUBLANES, :], ci.shape)

        for j in range(seg):
            pr, pi = _cmul(ar, ai, hr, hi)
            hr = pr + xr[tile(j), :]
            hi = pi + xi[tile(j), :]
            xr[tile(j), :] = hr
            xi[tile(j), :] = hi
        yss[:, OUT_TILE * o:OUT_TILE * (o + 1)] = _c_projection(xr, xi, cc_ref, o)

    ysn = _ssm_post(yss[...], scur[...], d_ref[...], wglu_ref, gs_ref[...])
    mix_ref[:, D_CONV:D_CONV + D_SSM] = jnp.dot(
        permt_ref[...], ysn.astype(BF16), preferred_element_type=F32).astype(BF16)

    @pl.when(ends_sequence)
    def _():
        hre_ref[0] = hsr[0:1, :]
        him_ref[0] = hsi[0:1, :]


def _front_prompt(x, batch, seq, nm, w_in, perm, cwb, cb, lg, lb, gc, bre, bim, cc, tab, dvec, wglu, gs, tc):
    nc = seq // tc
    total = batch * nc
    vec = _resident((1, D_CONV))
    mixed = lambda g: jnp.maximum(g - 1, 0)
    return pl.pallas_call(
        functools.partial(_front_prompt_kernel, tc=tc, nc=nc),
        out_shape=(
            jax.ShapeDtypeStruct((batch * seq, D_MODEL), BF16),
            jax.ShapeDtypeStruct((batch, HIST, D_CONV), F32),
            jax.ShapeDtypeStruct((batch, 1, N_STATE), F32),
            jax.ShapeDtypeStruct((batch, 1, N_STATE), F32),
        ),
        grid=(total + 1,),
        in_specs=[
            pl.BlockSpec((tc, D_MODEL), lambda g: (jnp.minimum(g, total - 1), 0)),
            _resident((1, D_MODEL)),
            _resident((D_IN // STATE_TILE, D_MODEL, STATE_TILE)),
            _resident((tc, tc)),
            _resident((tc, tc)),
            _resident(cwb.shape),
            vec, vec, vec, vec,
            _resident(bre.shape),
            _resident(bim.shape),
            _resident(cc.shape),
            _resident(tab.shape),
            vec,
            _resident((D_SSM, D_SSM)),
            vec,
        ],
        out_specs=(
            pl.BlockSpec((tc, D_MODEL), lambda g: (mixed(g), 0)),
            pl.BlockSpec((1, HIST, D_CONV), lambda g: (mixed(g) // nc, 0, 0)),
            pl.BlockSpec((1, 1, N_STATE), lambda g: (mixed(g) // nc, 0, 0)),
            pl.BlockSpec((1, 1, N_STATE), lambda g: (mixed(g) // nc, 0, 0)),
        ),
        scratch_shapes=[
            pltpu.VMEM((D_IN // STATE_TILE, tc, STATE_TILE), F32),
            pltpu.VMEM((tc, D_MODEL), BF16),
            pltpu.VMEM((HIST * SUBLANES + tc, D_CONV), F32),
            pltpu.VMEM((tc, D_CONV), F32),
            pltpu.VMEM((tc, D_CONV), F32),
            pltpu.VMEM((tc, D_SSM), F32),
            pltpu.VMEM((tc, STATES_PER_OUT_TILE), F32),
            pltpu.VMEM((tc, STATES_PER_OUT_TILE), F32),
            pltpu.VMEM((tc, STATES_PER_OUT_TILE), F32),
            pltpu.VMEM((tc, STATES_PER_OUT_TILE), F32),
            pltpu.VMEM((SUBLANES, N_STATE), F32),
            pltpu.VMEM((SUBLANES, N_STATE), F32),
            pltpu.VMEM((tc, D_SSM), F32),
        ],
        compiler_params=pltpu.CompilerParams(
            dimension_semantics=("arbitrary",), vmem_limit_bytes=VMEM_LIMIT),
        name="front_prompt",
    )(x, nm, w_in, perm, perm.T, cwb, cb, lg, lb, gc, bre, bim, cc, tab, dvec, wglu, gs)


def _front_sample_kernel(x_ref, sc_ref, h0r_ref, h0i_ref, nm_ref, win_ref,
                         cwb_ref, cb_ref, lg_ref, lb_ref, gc_ref,
                         bre_ref, bim_ref, cc_ref, tab_ref, d_ref, wglu_ref, gs_ref,
                         mix_ref, nc_ref, hre_ref, him_ref,
                         xns, ucur, scur, xr, xi, yss, *, steps, tb):
    def at(t, width):
        return slice(t * width, (t + 1) * width)

    for t in range(steps):
        x = x_ref[t]
        xns[at(t, tb), :] = (x * _rms_scale(x) * nm_ref[...]).astype(BF16)
    per = D_CONV // STATE_TILE
    for k in range(per):
        cols = at(k, STATE_TILE)
        v = jnp.dot(xns[...], win_ref[k], preferred_element_type=F32)
        gt = jnp.dot(xns[...], win_ref[per + k], preferred_element_type=F32)
        ucur[:, cols] = v * _sigmoid(gt)
        scur[:, cols] = jnp.dot(xns[...], win_ref[2 * per + k], preferred_element_type=F32)

    def window(j):
        return sc_ref[j] if j < HIST else ucur[at(j - HIST, tb), :]

    for t in range(steps):
        acc = jnp.zeros((tb // SUBLANES, SUBLANES, D_CONV), F32)
        for k in range(CONV_WIDTH):
            acc = acc + window(t + k).reshape(tb // SUBLANES, SUBLANES, D_CONV) * cwb_ref[k][None]
        ycn = _conv_post(acc.reshape(tb, D_CONV) + cb_ref[...], lg_ref[...], lb_ref[...], gc_ref[...])
        mix_ref[t, :, 0:D_CONV] = ycn.astype(BF16)
    for j in range(HIST):
        nc_ref[j] = window(steps + j)

    s = scur[...]
    sb = s.astype(BF16)
    for o in range(N_OUT_TILES):
        _b_projection(sb, bre_ref, bim_ref, xr, xi, o)
        lanes = slice(STATES_PER_OUT_TILE * o, STATES_PER_OUT_TILE * (o + 1))
        ar = tab_ref[T_AR, 0:1, lanes]
        ai = tab_ref[T_AI, 0:1, lanes]
        hr = h0r_ref[:, lanes]
        hi = h0i_ref[:, lanes]
        for t in range(steps):
            rs = slice(t * tb, (t + 1) * tb)
            hr, hi = ar * hr - ai * hi + xr[rs, :], ar * hi + ai * hr + xi[rs, :]
            xr[rs, :] = hr
            xi[rs, :] = hi
        hre_ref[:, lanes] = hr
        him_ref[:, lanes] = hi
        yss[:, OUT_TILE * o:OUT_TILE * (o + 1)] = _c_projection(xr, xi, cc_ref, o)

    ysn = _ssm_post(yss[...], s, d_ref[...], wglu_ref, gs_ref[...]).astype(BF16)
    for t in range(steps):
        mix_ref[t, :, D_CONV:D_MODEL] = ysn[at(t, tb), :]


def _front_sample(x, sc, h0r, h0i, nm, w_in, cwb, cb, lg, lb, gc, bre, bim, cc, tab, dvec, wglu, gs, tb):
    steps, batch, _ = x.shape
    vec = _resident((1, D_CONV))
    rows = lambda width: pl.BlockSpec((tb, width), lambda i: (i, 0))
    slabs = lambda n, width: pl.BlockSpec((n, tb, width), lambda i: (0, i, 0))
    return pl.pallas_call(
        functools.partial(_front_sample_kernel, steps=steps, tb=tb),
        out_shape=(
            jax.ShapeDtypeStruct((steps, batch, D_MODEL), BF16),
            jax.ShapeDtypeStruct((HIST, batch, D_CONV), F32),
            jax.ShapeDtypeStruct((batch, N_STATE), F32),
            jax.ShapeDtypeStruct((batch, N_STATE), F32),
        ),
        grid=(batch // tb,),
        in_specs=[
            slabs(steps, D_MODEL),
            slabs(HIST, D_CONV),
            rows(N_STATE), rows(N_STATE),
            _resident((1, D_MODEL)),
            _resident(w_in.shape),
            _resident(cwb.shape),
            vec, vec, vec, vec,
            _resident(bre.shape),
            _resident(bim.shape),
            _resident(cc.shape),
            _resident(tab.shape),
            vec,
            _resident((D_SSM, D_SSM)),
            vec,
        ],
        out_specs=(
            slabs(steps, D_MODEL),
            slabs(HIST, D_CONV),
            rows(N_STATE), rows(N_STATE),
        ),
        scratch_shapes=[
            pltpu.VMEM((steps * tb, D_MODEL), BF16),
            pltpu.VMEM((steps * tb, D_CONV), F32),
            pltpu.VMEM((steps * tb, D_SSM), F32),
            pltpu.VMEM((steps * tb, STATES_PER_OUT_TILE), F32),
            pltpu.VMEM((steps * tb, STATES_PER_OUT_TILE), F32),
            pltpu.VMEM((steps * tb, D_SSM), F32),
        ],
        compiler_params=pltpu.CompilerParams(
            dimension_semantics=("arbitrary",), vmem_limit_bytes=VMEM_LIMIT),
        name="front_sample",
    )(x, sc, h0r, h0i, nm, w_in, cwb, cb, lg, lb, gc, bre, bim, cc, tab, dvec, wglu, gs)


def _ssm_tables(a_re, a_im, log_dt, b_re, b_im, c_re, c_im, seg):
    ar = a_re.astype(F32)
    ai = a_im.astype(F32)
    dt = jnp.exp(log_dt.astype(F32))[:, None]

    mag = jnp.exp(dt * ar)
    ab_re = mag * jnp.cos(dt * ai)
    ab_im = mag * jnp.sin(dt * ai)
    den = ar * ar + ai * ai
    nr = ab_re - 1.0
    ni = ab_im
    k_re = (nr * ar + ni * ai) / den
    k_im = (ni * ar - nr * ai) / den
    br = b_re.astype(F32)
    bi = b_im.astype(F32)
    bb_re = k_re[..., None] * br - k_im[..., None] * bi
    bb_im = k_re[..., None] * bi + k_im[..., None] * br

    n = (jnp.arange(SUBLANES + 1, dtype=F32) * seg)[:, None, None]
    pmag = jnp.exp(n * dt * ar)
    pw_re = (pmag * jnp.cos(n * dt * ai)).reshape(SUBLANES + 1, N_STATE)
    pw_im = (pmag * jnp.sin(n * dt * ai)).reshape(SUBLANES + 1, N_STATE)
    row = jnp.arange(SUBLANES)[:, None]
    tabs = []
    for shift in (1, 2, 4):
        keep = row >= shift
        tabs += [jnp.where(keep, pw_re[shift][None, :], 0.0), jnp.where(keep, pw_im[shift][None, :], 0.0)]
    tabs += [pw_re[:SUBLANES], pw_im[:SUBLANES], pw_re[1:], pw_im[1:]]
    tabs += [jnp.broadcast_to(ab_re.reshape(1, N_STATE), (SUBLANES, N_STATE)),
             jnp.broadcast_to(ab_im.reshape(1, N_STATE), (SUBLANES, N_STATE))]
    tab = jnp.stack(tabs).astype(F32)

    groups_per_tile = STATE_TILE // SSM_STATE
    eye = jnp.eye(groups_per_tile, dtype=F32)

    def b_tiles(bb):
        bt = jnp.transpose(bb, (0, 2, 1)).reshape(N_STATE_TILES, groups_per_tile, SSM_GROUP_CH, SSM_STATE)
        blk = bt[:, :, :, None, :] * eye[None, :, None, :, None]
        blk = blk.reshape(N_STATE_TILES, groups_per_tile * SSM_GROUP_CH, STATE_TILE)
        halves = B_KSLICE // (groups_per_tile * SSM_GROUP_CH)
        sel = (jnp.arange(N_STATE_TILES)[:, None] % halves) == jnp.arange(halves)[None, :]
        out = blk[:, None, :, :] * sel[:, :, None, None].astype(F32)
        return out.reshape(N_STATE_TILES, B_KSLICE, STATE_TILE).astype(BF16)

    groups_per_out = OUT_TILE // SSM_GROUP_CH
    eye_o = jnp.eye(groups_per_out, dtype=F32)

    def c_rows(cm):
        ct = jnp.transpose(cm.astype(F32), (0, 2, 1)).reshape(N_OUT_TILES, groups_per_out, SSM_STATE, SSM_GROUP_CH)
        blk = ct[:, :, :, None, :] * eye_o[None, :, None, :, None]
        return blk.reshape(N_OUT_TILES, STATES_PER_OUT_TILE, OUT_TILE)

    cc = jnp.concatenate([c_rows(c_re), -c_rows(c_im)], axis=1).astype(BF16)
    return tab, b_tiles(bb_re), b_tiles(bb_im), cc


def _segment_permutation(tc):
    seg = tc // SUBLANES
    r = jnp.arange(tc)
    src = (r % SUBLANES) * seg + r // SUBLANES
    return (src[:, None] == jnp.arange(tc)[None, :]).astype(BF16)


def kernel(x_prompt, x_sample, state_conv, state_ssm_re, state_ssm_im, norm_mix, w_in, conv_w, conv_b, conv_ln_g, conv_ln_b, ssm_A_re, ssm_A_im, ssm_log_dt, ssm_B_re, ssm_B_im, ssm_C_re, ssm_C_im, ssm_D, w_glu, gnorm_conv, gnorm_ssm, w_out, norm_ffn, w_ffn_gate, w_ffn_up, w_ffn_down, norm_final):
    depth = norm_mix.shape[0]
    batch, seq, _ = x_prompt.shape
    dbatch, dsteps, _ = x_sample.shape
    tc = 256
    assert HIST <= tc // SUBLANES and seq % tc == 0

    hp = x_prompt.reshape(batch * seq, D_MODEL)
    hs = jnp.transpose(x_sample, (1, 0, 2)).reshape(dsteps * dbatch, D_MODEL)
    perm = _segment_permutation(tc)

    pc, pr, pi_, sc, sr, si = [], [], [], [], [], []
    for d in range(depth):
        tab, bre, bim, cc = _ssm_tables(ssm_A_re[d], ssm_A_im[d], ssm_log_dt[d],
                                        ssm_B_re[d], ssm_B_im[d], ssm_C_re[d], ssm_C_im[d],
                                        seg=tc // SUBLANES)
        vec = lambda a: a[d].reshape(1, -1).astype(F32)
        w_in_slabs = jnp.transpose(
            w_in[d].reshape(D_MODEL, D_IN // STATE_TILE, STATE_TILE), (1, 0, 2)).astype(BF16)
        wglu_b = w_glu[d].astype(BF16)
        wo_b = w_out[d].astype(BF16)
        wg_b = w_ffn_gate[d].astype(BF16)
        wu_b = w_ffn_up[d].astype(BF16)
        wd_b = w_ffn_down[d].astype(BF16)
        cwb = jnp.broadcast_to(conv_w[d].astype(F32)[:, None, :], (CONV_WIDTH, SUBLANES, D_CONV))
        mixer_params = (cwb, vec(conv_b), vec(conv_ln_g), vec(conv_ln_b), vec(gnorm_conv),
                        bre, bim, cc, tab, vec(ssm_D), wglu_b, vec(gnorm_ssm))
        last = d == depth - 1
        nfin = norm_final.reshape(1, -1).astype(F32)

        mix, cst, hre, him = _front_prompt(hp, batch, seq, vec(norm_mix), w_in_slabs, perm, *mixer_params, tc=tc)
        hp = _ffn(mix, hp, wo_b, vec(norm_ffn), wg_b, wu_b, wd_b, nfin, tm=512, final_norm=last)
        pc.append(cst)
        pr.append(hre.reshape(batch, N_GROUPS, SSM_STATE))
        pi_.append(him.reshape(batch, N_GROUPS, SSM_STATE))

        mix, nconv, hre, him = _front_sample(
            hs.reshape(dsteps, dbatch, D_MODEL), jnp.transpose(state_conv[d], (1, 0, 2)),
            state_ssm_re[d].reshape(dbatch, N_STATE), state_ssm_im[d].reshape(dbatch, N_STATE),
            vec(norm_mix), w_in_slabs, *mixer_params, tb=32)
        hs = _ffn(mix.reshape(dsteps * dbatch, D_MODEL), hs, wo_b, vec(norm_ffn), wg_b, wu_b, wd_b, nfin,
                  tm=512, final_norm=last)
        sc.append(jnp.transpose(nconv, (1, 0, 2)))
        sr.append(hre.reshape(dbatch, N_GROUPS, SSM_STATE))
        si.append(him.reshape(dbatch, N_GROUPS, SSM_STATE))

    stack = lambda parts: parts[0][None] if len(parts) == 1 else jnp.stack(parts)
    y_prompt = hp.reshape(batch, seq, D_MODEL)
    y_sample = jnp.transpose(hs.reshape(dsteps, dbatch, D_MODEL), (1, 0, 2))
    return (y_prompt, y_sample, stack(pc), stack(pr), stack(pi_), stack(sc), stack(sr), stack(si))
```
